```python
import jax, jax.numpy as jnp
from jax import lax
import numpy as np

D_MODEL = 1024
BATCH = 8
SEQ = 4096
DEPTH = 2

GRID_W = 64
D_MIX = D_MODEL
GROUP_W = D_MIX // 4
HEAD_DIM = 64
CONV_A_WIDTH = 31
GQA_HEADS = GROUP_W // HEAD_DIM
GQA_KV_HEADS = 2
CHUNK = 128
SGU_GROUPS = GROUP_W // HEAD_DIM
SGU_GROUP_DIM = GROUP_W // SGU_GROUPS
MLA_HEADS = GROUP_W // HEAD_DIM
MLA_Q_LORA = 3 * D_MODEL // 16
MLA_KV_LORA = D_MODEL // 8
MLA_NOPE = 64
MLA_ROPE = 32
MLA_V = GROUP_W // MLA_HEADS
Q_BLOCK = 128
ROPE_THETA = 10000.0
D_FF = 2816
FFN_CONV_WIDTH = 3
DEEPNORM_ALPHA = (2 * DEPTH) ** 0.25
DEEPNORM_BETA = (8 * DEPTH) ** -0.25
LN_EPS = 1e-5
RMS_EPS = 1e-6

SPLIT_SIZES = (2 * GROUP_W,
               GQA_HEADS * HEAD_DIM,
               GQA_KV_HEADS * HEAD_DIM,
               GQA_KV_HEADS * HEAD_DIM,
               2 * GROUP_W,
               MLA_Q_LORA,
               MLA_KV_LORA,
               MLA_ROPE)
D_IN_PROJ = sum(SPLIT_SIZES)
SPLIT_POINTS = [sum(SPLIT_SIZES[:i + 1]) for i in range(len(SPLIT_SIZES) - 1)]

kernel_name = 'hymba_style_hybrid_encoder'


def layer_norm(x, g, b):
    xf = x.astype(jnp.float32)
    mu = jnp.mean(xf, axis=-1, keepdims=True)
    xc = xf - mu
    var = jnp.mean(xc * xc, axis=-1, keepdims=True)
    return (xc * lax.rsqrt(var + LN_EPS) * g + b).astype(x.dtype)


def rms_norm(x, g):
    xf = x.astype(jnp.float32)
    ms = jnp.mean(xf * xf, axis=-1, keepdims=True)
    return (xf * lax.rsqrt(ms + RMS_EPS) * g).astype(x.dtype)


def depthwise_conv(x, w, b):
    k = w.shape[0]
    c = x.shape[-1]
    y = lax.conv_general_dilated(x, w[:, None, :], window_strides=(1,),
                                 padding=[(k // 2, k // 2)],
                                 dimension_numbers=('NWC', 'WIO', 'NWC'),
                                 feature_group_count=c)
    return y + b


def rope_1d(x, pos):
    d = x.shape[-1]
    half = d // 2
    inv_freq = ROPE_THETA ** (-jnp.arange(half, dtype=jnp.float32) / half)
    ang = pos[:, None] * inv_freq[None, :]
    cos = jnp.cos(ang)[:, None, :]
    sin = jnp.sin(ang)[:, None, :]
    xf = x.astype(jnp.float32)
    x1, x2 = xf[..., :half], xf[..., half:]
    return jnp.concatenate([x1 * cos - x2 * sin, x2 * cos + x1 * sin], axis=-1).astype(x.dtype)


def rope_2d(x, row, col):
    h = x.shape[-1] // 2
    return jnp.concatenate([rope_1d(x[..., :h], row), rope_1d(x[..., h:], col)], axis=-1)


def blocked_attention(q, k, v, scale):
    bsz, seq, hk, g, dk = q.shape
    nb = seq // Q_BLOCK
    qb = q.reshape(bsz, nb, Q_BLOCK, hk, g, dk).transpose(1, 0, 2, 3, 4, 5)

    def one_block(qi):
        s = jnp.einsum('bqhgd,bshd->bhgqs', qi, k, preferred_element_type=jnp.float32) * scale
        p = jax.nn.softmax(s, axis=-1).astype(v.dtype)
        return jnp.einsum('bhgqs,bshd->bqhgd', p, v)

    out = lax.map(one_block, qb)
    return out.transpose(1, 0, 2, 3, 4, 5).reshape(bsz, seq, hk, g, v.shape[-1])


def setup_inputs(seed: int = 0) -> dict:
    key = jax.random.key(seed)
    ks = iter(jax.random.split(key, 32))

    def nrm(shape, scale):
        return scale * jax.random.normal(next(ks), shape, dtype=jnp.float32)

    L = DEPTH
    return {
        'x': nrm((BATCH, SEQ, D_MODEL), 1.0),
        'ln_in_g': 1.0 + nrm((D_MODEL,), 0.02),
        'ln_in_b': nrm((D_MODEL,), 0.02),
        'w_in': nrm((L, D_MODEL, D_IN_PROJ), D_MODEL ** -0.5),
        'conv_a_w': nrm((L, CONV_A_WIDTH, GROUP_W), CONV_A_WIDTH ** -0.5),
        'conv_a_b': nrm((L, GROUP_W), 0.02),
        'ln_a_g': 1.0 + nrm((L, GROUP_W), 0.02),
        'ln_a_b': nrm((L, GROUP_W), 0.02),
        'qk_norm_q': 1.0 + nrm((L, HEAD_DIM), 0.02),
        'qk_norm_k': 1.0 + nrm((L, HEAD_DIM), 0.02),
        'sgu_ln_g': 1.0 + nrm((L, GROUP_W), 0.02),
        'sgu_ln_b': nrm((L, GROUP_W), 0.02),
        'sgu_w': nrm((L, SGU_GROUPS, CHUNK, CHUNK), CHUNK ** -0.5),
        'sgu_b': 1.0 + nrm((L, SGU_GROUPS, CHUNK), 0.02),
        'mla_q_norm': 1.0 + nrm((L, MLA_Q_LORA), 0.02),
        'mla_w_uq': nrm((L, MLA_Q_LORA, MLA_HEADS * (MLA_NOPE + MLA_ROPE)), MLA_Q_LORA ** -0.5),
        'mla_kv_norm': 1.0 + nrm((L, MLA_KV_LORA), 0.02),
        'mla_w_ukv': nrm((L, MLA_KV_LORA, MLA_HEADS * (MLA_NOPE + MLA_V)), MLA_KV_LORA ** -0.5),
        'w_out': nrm((L, D_MIX, D_MODEL), DEEPNORM_BETA * D_MIX ** -0.5),
        'ln_mix_g': 1.0 + nrm((L, D_MODEL), 0.02),
        'ln_mix_b': nrm((L, D_MODEL), 0.02),
        'ffn_w_up': nrm((L, D_MODEL, 2 * D_FF), D_MODEL ** -0.5),
        'ffn_conv_w': nrm((L, FFN_CONV_WIDTH, 2 * D_FF), FFN_CONV_WIDTH ** -0.5),
        'ffn_conv_b': nrm((L, 2 * D_FF), 0.02),
        'ffn_w_down': nrm((L, D_FF, D_MODEL), DEEPNORM_BETA * D_FF ** -0.5),
        'ln_ffn_g': 1.0 + nrm((L, D_MODEL), 0.02),
        'ln_ffn_b': nrm((L, D_MODEL), 0.02),
    }


def reference(x, ln_in_g, ln_in_b, w_in, conv_a_w, conv_a_b, ln_a_g, ln_a_b,
              qk_norm_q, qk_norm_k, sgu_ln_g, sgu_ln_b, sgu_w, sgu_b,
              mla_q_norm, mla_w_uq, mla_kv_norm, mla_w_ukv, w_out, ln_mix_g, ln_mix_b,
              ffn_w_up, ffn_conv_w, ffn_conv_b, ffn_w_down, ln_ffn_g, ln_ffn_b):
    bsz, seq, _ = x.shape
    rows = seq // GRID_W
    row = jnp.repeat(jnp.arange(rows, dtype=jnp.float32), GRID_W)
    col = jnp.tile(jnp.arange(GRID_W, dtype=jnp.float32), rows)

    h = layer_norm(x, ln_in_g, ln_in_b)
    for l in range(DEPTH):
        proj = h @ w_in[l]
        a_in, b_q, b_k, b_v, c_in, d_cq, d_ckv, d_kr = jnp.split(proj, SPLIT_POINTS, axis=-1)

        a = a_in[..., :GROUP_W] * jax.nn.sigmoid(a_in[..., GROUP_W:])
        a = depthwise_conv(a, conv_a_w[l], conv_a_b[l])
        o_a = jax.nn.silu(layer_norm(a, ln_a_g[l], ln_a_b[l]))

        q = rms_norm(b_q.reshape(bsz, seq, GQA_HEADS, HEAD_DIM), qk_norm_q[l])
        k = rms_norm(b_k.reshape(bsz, seq, GQA_KV_HEADS, HEAD_DIM), qk_norm_k[l])
        v = b_v.reshape(bsz, seq, GQA_KV_HEADS, HEAD_DIM)
        q = rope_2d(q, row, col).reshape(bsz, seq, GQA_KV_HEADS, GQA_HEADS // GQA_KV_HEADS, HEAD_DIM)
        k = rope_2d(k, row, col)
        o_b = blocked_attention(q, k, v, HEAD_DIM ** -0.5).reshape(bsz, seq, GROUP_W)

        c = jax.nn.gelu(c_in)
        u, sv = c[..., :GROUP_W], c[..., GROUP_W:]
        sv = layer_norm(sv, sgu_ln_g[l], sgu_ln_b[l])
        sv = sv.reshape(bsz, seq // CHUNK, CHUNK, SGU_GROUPS, SGU_GROUP_DIM)
        sv = jnp.einsum('gpq,bnqgc->bnpgc', sgu_w[l], sv) + sgu_b[l].T[:, :, None]
        o_c = u * sv.reshape(bsz, seq, GROUP_W)

        qd = (rms_norm(d_cq, mla_q_norm[l]) @ mla_w_uq[l]).reshape(bsz, seq, MLA_HEADS, MLA_NOPE + MLA_ROPE)
        kvd = (rms_norm(d_ckv, mla_kv_norm[l]) @ mla_w_ukv[l]).reshape(bsz, seq, MLA_HEADS, MLA_NOPE + MLA_V)
        q_nope, q_rope = qd[..., :MLA_NOPE], qd[..., MLA_NOPE:]
        k_nope, v_d = kvd[..., :MLA_NOPE], kvd[..., MLA_NOPE:]
        k_rope = rope_2d(d_kr[:, :, None, :], row, col)
        q_full = jnp.concatenate([q_nope, rope_2d(q_rope, row, col)], axis=-1)
        k_full = jnp.concatenate([k_nope, jnp.broadcast_to(k_rope, (bsz, seq, MLA_HEADS, MLA_ROPE))], axis=-1)
        o_d = blocked_attention(q_full[:, :, :, None, :], k_full, v_d,
                                (MLA_NOPE + MLA_ROPE) ** -0.5).reshape(bsz, seq, GROUP_W)

        mix = jnp.concatenate([o_a, o_b, o_c, o_d], axis=-1) @ w_out[l]
        h = layer_norm(DEEPNORM_ALPHA * h + mix, ln_mix_g[l], ln_mix_b[l])

        up = depthwise_conv(h @ ffn_w_up[l], ffn_conv_w[l], ffn_conv_b[l])
        f = (jax.nn.silu(up[..., :D_FF]) * up[..., D_FF:]) @ ffn_w_down[l]
        h = layer_norm(DEEPNORM_ALPHA * h + f, ln_ffn_g[l], ln_ffn_b[l])
    return h
```

```python
from functools import partial

import jax
import jax.numpy as jnp
from jax import lax
from jax.experimental import pallas as pl
from jax.experimental.pallas import tpu as pltpu

F32 = jnp.float32
BF16 = jnp.bfloat16

D_MODEL = 1024
GRID_W = 64
GROUP_W = 256
HEAD_DIM = 64
CONV_A_WIDTH = 31
CHUNK = 128
SGU_GROUPS = 4
MLA_Q_LORA = 192
MLA_KV_LORA = 128
MLA_NOPE = 64
MLA_ROPE = 32
D_FF = 2816
DEPTH = 2
ROPE_THETA = 10000.0
DEEPNORM_ALPHA = (2 * DEPTH) ** 0.25
LN_EPS = 1e-5
RMS_EPS = 1e-6

LANES = 128
BF16_SUBLANES = 16
VMEM_LIMIT_BYTES = 52 * 1024 * 1024

TOKEN_TILE = 512
ATT_Q_TILE = 256
ATT_KEY_CHUNK = 512
FF_CHUNK = 256
CONV_HALO = 16
FFN_HALO = 16

COL_A = 0
COL_Q = 512
COL_K = 1024
COL_V = 1152
COL_C = 1408
COL_CQ = 1920
COL_CKV = 2176
COL_KR = 2304
IN_COLS = 2432


def _dot(a, b):
    return jnp.dot(a, b, preferred_element_type=F32)


def _layer_norm(x, g, b):
    mu = jnp.mean(x, axis=-1, keepdims=True)
    xc = x - mu
    var = jnp.mean(xc * xc, axis=-1, keepdims=True)
    return xc * lax.rsqrt(var + LN_EPS) * g + b


def _rope(x, cos, sin_signed, half):
    lane = lax.broadcasted_iota(jnp.int32, x.shape, 1)
    first = (lane & (2 * half - 1)) < half
    partner = jnp.where(first, pltpu.roll(x, LANES - half, 1), pltpu.roll(x, half, 1))
    return x * cos + partner * sin_signed


def _ln_kernel(x_ref, g_ref, b_ref, o_ref):
    o_ref[...] = _layer_norm(x_ref[...], g_ref[...], b_ref[...])


def _input_ln(x, g, b):
    t = x.shape[0]
    return pl.pallas_call(
        _ln_kernel,
        grid=(t // TOKEN_TILE,),
        in_specs=[pl.BlockSpec((TOKEN_TILE, D_MODEL), lambda i: (i, 0)),
                  pl.BlockSpec((1, D_MODEL), lambda i: (0, 0)),
                  pl.BlockSpec((1, D_MODEL), lambda i: (0, 0))],
        out_specs=pl.BlockSpec((TOKEN_TILE, D_MODEL), lambda i: (i, 0)),
        out_shape=jax.ShapeDtypeStruct((t, D_MODEL), F32),
        compiler_params=pltpu.CompilerParams(dimension_semantics=("parallel",)),
        name="input_ln",
    )(x, g, b)


def _inproj_kernel(h_ref, w_ref, cos_g_ref, sin_g_ref, cos_m_ref, sin_m_ref,
                   gq_ref, gk_ref, sgu_g_ref, sgu_b_ref, sgu_w_ref, sgu_bias_ref,
                   gcq_ref, w_uq_ref, gckv_ref, w_ukv_ref,
                   a_ref, qb_ref, kb_ref, vb_ref, oc_ref, qd_ref, kd_ref, vd_ref):
    tm = h_ref.shape[0]
    hb = h_ref[...].astype(BF16)

    def proj(col, width):
        return _dot(hb, w_ref[:, col:col + width])

    pa = proj(COL_A, 2 * GROUP_W)
    a_ref[...] = pa[:, :GROUP_W] * jax.nn.sigmoid(pa[:, GROUP_W:])

    cos_g, sin_g = cos_g_ref[...], sin_g_ref[...]
    cos_m, sin_m = cos_m_ref[...], sin_m_ref[...]

    pq = proj(COL_Q, 4 * LANES)
    for hd in range(4):
        x = pq[:, hd * LANES:(hd + 1) * LANES]
        ms = jnp.sum(x * x, axis=-1, keepdims=True) * (1.0 / HEAD_DIM)
        xn = x * lax.rsqrt(ms + RMS_EPS) * gq_ref[...]
        xr = _rope(xn, cos_g, sin_g, HEAD_DIM // 4) * (HEAD_DIM ** -0.5)
        qb_ref[:, hd * LANES:(hd + 1) * LANES] = xr.astype(BF16)

    pk = proj(COL_K, LANES)
    low = lax.broadcasted_iota(jnp.int32, pk.shape, 1) < HEAD_DIM
    sq = pk * pk
    ms_lo = jnp.sum(jnp.where(low, sq, 0.0), axis=-1, keepdims=True) * (1.0 / HEAD_DIM)
    ms_hi = jnp.sum(jnp.where(low, 0.0, sq), axis=-1, keepdims=True) * (1.0 / HEAD_DIM)
    rs = jnp.where(low, lax.rsqrt(ms_lo + RMS_EPS), lax.rsqrt(ms_hi + RMS_EPS))
    kb_ref[...] = _rope(pk * rs * gk_ref[...], cos_g, sin_g, HEAD_DIM // 4).astype(BF16)

    vb_ref[...] = proj(COL_V, 2 * LANES).astype(BF16)

    pc = jax.nn.gelu(proj(COL_C, 2 * GROUP_W), approximate=True)
    u = pc[:, :GROUP_W]
    sv = _layer_norm(pc[:, GROUP_W:], sgu_g_ref[...], sgu_b_ref[...]).astype(BF16)
    lane = lax.broadcasted_iota(jnp.int32, (CHUNK, GROUP_W), 1)
    for n in range(tm // CHUNK):
        rows = slice(n * CHUNK, (n + 1) * CHUNK)
        r = _dot(sgu_w_ref[...], sv[rows, :])
        gate = r[3 * CHUNK:4 * CHUNK, :]
        for g in (2, 1, 0):
            gate = jnp.where(lane < (g + 1) * HEAD_DIM, r[g * CHUNK:(g + 1) * CHUNK, :], gate)
        oc_ref[rows, :] = (u[rows, :] * (gate + sgu_bias_ref[...])).astype(BF16)

    pcq = proj(COL_CQ, 2 * LANES)
    ms = jnp.sum(pcq * pcq, axis=-1, keepdims=True) * (1.0 / MLA_Q_LORA)
    cqn = (pcq * lax.rsqrt(ms + RMS_EPS) * gcq_ref[...]).astype(BF16)
    scale_d = (MLA_NOPE + MLA_ROPE) ** -0.5
    for hd in range(4):
        qd = _dot(cqn, w_uq_ref[:, hd * 2 * LANES:(hd + 1) * 2 * LANES])
        qd_ref[:, hd * 2 * LANES:hd * 2 * LANES + LANES] = (qd[:, :LANES] * scale_d).astype(BF16)
        qr = _rope(qd[:, LANES:], cos_m, sin_m, MLA_ROPE // 4) * scale_d
        qd_ref[:, hd * 2 * LANES + LANES:(hd + 1) * 2 * LANES] = qr.astype(BF16)

    pckv = proj(COL_CKV, LANES)
    ms = jnp.mean(pckv * pckv, axis=-1, keepdims=True)
    ckvn = (pckv * lax.rsqrt(ms + RMS_EPS) * gckv_ref[...]).astype(BF16)
    kvd = _dot(ckvn, w_ukv_ref[...])
    kr = _rope(proj(COL_KR, LANES), cos_m, sin_m, MLA_ROPE // 4).astype(BF16)
    for pr in range(2):
        kd_ref[:, pr * 2 * LANES:pr * 2 * LANES + LANES] = kvd[:, pr * LANES:(pr + 1) * LANES].astype(BF16)
        kd_ref[:, pr * 2 * LANES + LANES:(pr + 1) * 2 * LANES] = kr
    vd_ref[...] = kvd[:, 2 * LANES:].astype(BF16)


def _inproj(h, seq, w_in_e, tabs, lw):
    t = h.shape[0]
    tm = TOKEN_TILE
    spt = seq // tm
    tok = lambda width: pl.BlockSpec((tm, width), lambda i: (i, 0))
    pos = pl.BlockSpec((tm, LANES), lambda i: (i % spt, 0))
    full = lambda arr: pl.BlockSpec(arr.shape, lambda i: (0,) * arr.ndim)
    consts = [lw["gq"], lw["gk"], lw["sgu_g"], lw["sgu_b"], lw["sgu_w"], lw["sgu_bias"],
              lw["gcq"], lw["w_uq"], lw["gckv"], lw["w_ukv"]]
    out_widths = [(GROUP_W, F32), (4 * LANES, BF16), (LANES, BF16), (2 * LANES, BF16), (GROUP_W, BF16),
                  (8 * LANES, BF16), (4 * LANES, BF16), (GROUP_W, BF16)]
    return pl.pallas_call(
        _inproj_kernel,
        grid=(t // tm,),
        in_specs=[tok(D_MODEL), full(w_in_e), pos, pos, pos, pos] + [full(c) for c in consts],
        out_specs=[tok(w) for w, _ in out_widths],
        out_shape=[jax.ShapeDtypeStruct((t, w), dt) for w, dt in out_widths],
        compiler_params=pltpu.CompilerParams(dimension_semantics=("parallel",),
                                             vmem_limit_bytes=VMEM_LIMIT_BYTES),
        name="inproj",
    )(h, w_in_e, tabs["cos_g"], tabs["sin_g"], tabs["cos_m"], tabs["sin_m"], *consts)


def _attention_kernel(q_ref, k_ref, v_ref, o_ref, s_ref):
    tq = q_ref.shape[1]
    kw = k_ref.shape[2]
    seq = k_ref.shape[1]
    q2 = jnp.concatenate([q_ref[0, :, :kw], q_ref[0, :, kw:]], axis=0)
    nchunk = seq // ATT_KEY_CHUNK
    groups = ATT_KEY_CHUNK // LANES

    m = None
    for c in range(nchunk):
        keys = k_ref[0, c * ATT_KEY_CHUNK:(c + 1) * ATT_KEY_CHUNK, :]
        s = lax.dot_general(q2, keys, (((1,), (1,)), ((), ())), preferred_element_type=F32)
        s_ref[:, c * ATT_KEY_CHUNK:(c + 1) * ATT_KEY_CHUNK] = s
        for g in range(groups):
            part = s[:, g * LANES:(g + 1) * LANES]
            m = part if m is None else jnp.maximum(m, part)
    m_row = jnp.max(m, axis=-1, keepdims=True)

    l = jnp.zeros((2 * tq, LANES), F32)
    acc = jnp.zeros((2 * tq, LANES), F32)
    for c in range(nchunk):
        p = jnp.exp(s_ref[:, c * ATT_KEY_CHUNK:(c + 1) * ATT_KEY_CHUNK] - m_row)
        for g in range(groups):
            l = l + p[:, g * LANES:(g + 1) * LANES]
        acc = acc + _dot(p.astype(BF16), v_ref[0, c * ATT_KEY_CHUNK:(c + 1) * ATT_KEY_CHUNK, :])
    out = acc / jnp.sum(l, axis=-1, keepdims=True)
    lane = lax.broadcasted_iota(jnp.int32, (tq, LANES), 1)
    o_ref[0] = jnp.where(lane < HEAD_DIM, out[:tq], out[tq:]).astype(o_ref.dtype)


def _attention(q, k, v, k_shared):
    bsz, seq, _ = q.shape
    kw = k.shape[2] if k_shared else k.shape[2] // 2
    tq = ATT_Q_TILE
    k_map = (lambda b, p, i: (b, 0, 0)) if k_shared else (lambda b, p, i: (b, 0, p))
    return pl.pallas_call(
        _attention_kernel,
        grid=(bsz, 2, seq // tq),
        in_specs=[pl.BlockSpec((1, tq, 2 * kw), lambda b, p, i: (b, i, p)),
                  pl.BlockSpec((1, seq, kw), k_map),
                  pl.BlockSpec((1, seq, LANES), lambda b, p, i: (b, 0, p))],
        out_specs=pl.BlockSpec((1, tq, LANES), lambda b, p, i: (b, i, p)),
        out_shape=jax.ShapeDtypeStruct((bsz, seq, 2 * LANES), BF16),
        scratch_shapes=[pltpu.VMEM((2 * tq, seq), F32)],
        compiler_params=pltpu.CompilerParams(dimension_semantics=("parallel", "parallel", "parallel"),
                                             vmem_limit_bytes=VMEM_LIMIT_BYTES),
        name="attention_shared_k" if k_shared else "attention",
    )(q, k, v)


def _mixout_kernel(a_ref, a_prev_ref, a_next_ref, cw_ref, cb_ref, ga_ref, ba_ref,
                   ob_ref, oc_ref, od_ref, w_ref, h_ref, g_ref, b_ref, o_ref, ext_ref, *, tiles_per_seq):
    tm = a_ref.shape[0]
    i = pl.program_id(0) % tiles_per_seq
    has_prev = jnp.where(i > 0, 1.0, 0.0).astype(F32)
    has_next = jnp.where(i < tiles_per_seq - 1, 1.0, 0.0).astype(F32)
    ext_ref[0:CONV_HALO, :] = a_prev_ref[...] * has_prev
    ext_ref[CONV_HALO:CONV_HALO + tm, :] = a_ref[...]
    ext_ref[CONV_HALO + tm:, :] = a_next_ref[...] * has_next
    half = CONV_A_WIDTH // 2
    acc = jnp.zeros((tm, GROUP_W), F32)
    for k in range(CONV_A_WIDTH):
        start = CONV_HALO - half + k
        acc = acc + ext_ref[start:start + tm, :] * cw_ref[k:k + 1, :]
    conv = acc + cb_ref[...]
    o_a = jax.nn.silu(_layer_norm(conv, ga_ref[...], ba_ref[...])).astype(BF16)

    mix = _dot(o_a, w_ref[0:GROUP_W, :])
    mix = mix + _dot(ob_ref[...], w_ref[GROUP_W:2 * GROUP_W, :])
    mix = mix + _dot(oc_ref[...], w_ref[2 * GROUP_W:3 * GROUP_W, :])
    mix = mix + _dot(od_ref[...], w_ref[3 * GROUP_W:, :])
    o_ref[...] = _layer_norm(DEEPNORM_ALPHA * h_ref[...] + mix, g_ref[...], b_ref[...])


def _mixout(a, o_b, o_c, o_d, h, seq, lw):
    t = h.shape[0]
    tm = TOKEN_TILE
    spt = seq // tm
    hb = tm // CONV_HALO
    last = t // CONV_HALO - 1
    tok = lambda width: pl.BlockSpec((tm, width), lambda i: (i, 0))
    full = lambda arr: pl.BlockSpec(arr.shape, lambda i: (0,) * arr.ndim)
    consts1 = [lw["conv_a_w"], lw["conv_a_b"], lw["ln_a_g"], lw["ln_a_b"]]
    consts2 = [lw["ln_mix_g"], lw["ln_mix_b"]]
    return pl.pallas_call(
        partial(_mixout_kernel, tiles_per_seq=spt),
        grid=(t // tm,),
        in_specs=[tok(GROUP_W),
                  pl.BlockSpec((CONV_HALO, GROUP_W), lambda i: (jnp.maximum(i * hb - 1, 0), 0)),
                  pl.BlockSpec((CONV_HALO, GROUP_W), lambda i: (jnp.minimum((i + 1) * hb, last), 0))]
                 + [full(c) for c in consts1]
                 + [tok(GROUP_W), tok(GROUP_W), tok(GROUP_W), full(lw["w_out"]), tok(D_MODEL)]
                 + [full(c) for c in consts2],
        out_specs=tok(D_MODEL),
        out_shape=jax.ShapeDtypeStruct((t, D_MODEL), F32),
        scratch_shapes=[pltpu.VMEM((tm + 2 * CONV_HALO, GROUP_W), F32)],
        compiler_params=pltpu.CompilerParams(dimension_semantics=("parallel",),
                                             vmem_limit_bytes=VMEM_LIMIT_BYTES),
        name="mixout",
    )(a, a, a, *consts1, o_b, o_c, o_d, lw["w_out"], h, *consts2)


def _ffn_kernel(h_ref, h_prev_ref, h_next_ref, w_up_ref, cw_ref, cb_ref, w_down_ref, g_ref, b_ref,
                o_ref, ext_ref, acc_ref, *, tiles_per_seq):
    tm = h_ref.shape[0]
    i = pl.program_id(0) % tiles_per_seq
    has_prev = jnp.where(i > 0, 1.0, 0.0).astype(F32)
    has_next = jnp.where(i < tiles_per_seq - 1, 1.0, 0.0).astype(F32)
    ext_ref[0:FFN_HALO, :] = (h_prev_ref[...] * has_prev).astype(BF16)
    ext_ref[FFN_HALO:FFN_HALO + tm, :] = h_ref[...].astype(BF16)
    ext_ref[FFN_HALO + tm:, :] = (h_next_ref[...] * has_next).astype(BF16)
    hext = ext_ref[...]

    def conv_up(col):
        up = _dot(hext, w_up_ref[:, col:col + FF_CHUNK])
        w = cw_ref[:, col:col + FF_CHUNK]
        return (up[FFN_HALO - 1:FFN_HALO - 1 + tm] * w[0:1] + up[FFN_HALO:FFN_HALO + tm] * w[1:2]
                + up[FFN_HALO + 1:FFN_HALO + 1 + tm] * w[2:3] + cb_ref[:, col:col + FF_CHUNK])

    for j in range(D_FF // FF_CHUNK):
        col = j * FF_CHUNK
        act = (jax.nn.silu(conv_up(col)) * conv_up(D_FF + col)).astype(BF16)
        part = _dot(act, w_down_ref[col:col + FF_CHUNK, :])
        if j == 0:
            acc_ref[...] = part
        else:
            acc_ref[...] += part
    o_ref[...] = _layer_norm(DEEPNORM_ALPHA * h_ref[...] + acc_ref[...], g_ref[...], b_ref[...])


def _ffn(h, seq, lw):
    t = h.shape[0]
    tm = TOKEN_TILE
    spt = seq // tm
    hb = tm // FFN_HALO
    last = t // FFN_HALO - 1
    tok = pl.BlockSpec((tm, D_MODEL), lambda i: (i, 0))
    full = lambda arr: pl.BlockSpec(arr.shape, lambda i: (0,) * arr.ndim)
    resident = lambda arr: pl.BlockSpec(arr.shape, lambda i: (0,) * arr.ndim, pipeline_mode=pl.Buffered(1))
    return pl.pallas_call(
        partial(_ffn_kernel, tiles_per_seq=spt),
        grid=(t // tm,),
        in_specs=[tok,
                  pl.BlockSpec((FFN_HALO, D_MODEL), lambda i: (jnp.maximum(i * hb - 1, 0), 0)),
                  pl.BlockSpec((FFN_HALO, D_MODEL), lambda i: (jnp.minimum((i + 1) * hb, last), 0)),
                  resident(lw["ffn_w_up"]), full(lw["ffn_conv_w"]), full(lw["ffn_conv_b"]),
                  resident(lw["ffn_w_down"]), full(lw["ln_ffn_g"]), full(lw["ln_ffn_b"])],
        out_specs=tok,
        out_shape=jax.ShapeDtypeStruct((t, D_MODEL), F32),
        scratch_shapes=[pltpu.VMEM((tm + 2 * FFN_HALO, D_MODEL), BF16),
                        pltpu.VMEM((tm, D_MODEL), F32)],
        compiler_params=pltpu.CompilerParams(dimension_semantics=("parallel",),
                                             vmem_limit_bytes=VMEM_LIMIT_BYTES),
        name="ffn",
    )(h, h, h, lw["ffn_w_up"], lw["ffn_conv_w"], lw["ffn_conv_b"], lw["ffn_w_down"],
      lw["ln_ffn_g"], lw["ln_ffn_b"])


def _rope_tables(seq):
    rows = seq // GRID_W
    row = jnp.repeat(jnp.arange(rows, dtype=F32), GRID_W)
    col = jnp.tile(jnp.arange(GRID_W, dtype=F32), rows)

    def tables(half):
        inv_freq = ROPE_THETA ** (-jnp.arange(half, dtype=F32) / half)
        ar = row[:, None] * inv_freq[None, :]
        ac = col[:, None] * inv_freq[None, :]
        cos = jnp.concatenate([jnp.cos(ar), jnp.cos(ar), jnp.cos(ac), jnp.cos(ac)], axis=-1)
        sin = jnp.concatenate([-jnp.sin(ar), jnp.sin(ar), -jnp.sin(ac), jnp.sin(ac)], axis=-1)
        return cos, sin

    cos64, sin64 = tables(HEAD_DIM // 4)
    cos32, sin32 = tables(MLA_ROPE // 4)
    pad = LANES - MLA_ROPE
    return {
        "cos_g": jnp.tile(cos64, (1, 2)), "sin_g": jnp.tile(sin64, (1, 2)),
        "cos_m": jnp.concatenate([cos32, jnp.ones((seq, pad), F32)], axis=-1),
        "sin_m": jnp.concatenate([sin32, jnp.zeros((seq, pad), F32)], axis=-1),
    }


def _layer_weights(l, w_in, conv_a_w, conv_a_b, ln_a_g, ln_a_b, qk_norm_q, qk_norm_k, sgu_ln_g, sgu_ln_b,
                   sgu_w, sgu_b, mla_q_norm, mla_w_uq, mla_kv_norm, mla_w_ukv, w_out, ln_mix_g, ln_mix_b,
                   ffn_w_up, ffn_conv_w, ffn_conv_b, ffn_w_down, ln_ffn_g, ln_ffn_b):
    w = w_in[l]
    z = lambda n: jnp.zeros((D_MODEL, n), F32)
    o = 2 * GROUP_W
    wq = w[:, o:o + 256]
    wk = w[:, o + 256:o + 384]
    wv = w[:, o + 384:o + 512]
    o += 512
    wc = w[:, o:o + 512]
    o += 512
    wcq = w[:, o:o + MLA_Q_LORA]
    wckv = w[:, o + MLA_Q_LORA:o + MLA_Q_LORA + MLA_KV_LORA]
    wkr = w[:, o + MLA_Q_LORA + MLA_KV_LORA:]
    hd = HEAD_DIM
    w_in_e = jnp.concatenate([
        w[:, :2 * GROUP_W],
        wq[:, 0:hd], z(hd), wq[:, hd:2 * hd], z(hd), z(hd), wq[:, 2 * hd:3 * hd], z(hd), wq[:, 3 * hd:],
        wk,
        wv[:, :hd], wv[:, :hd], wv[:, hd:], wv[:, hd:],
        wc,
        wcq, z(2 * LANES - MLA_Q_LORA),
        wckv,
        wkr, z(LANES - MLA_ROPE)], axis=1).astype(BF16)
    assert w_in_e.shape == (D_MODEL, IN_COLS)

    dq = MLA_NOPE + MLA_ROPE
    zq = lambda n: jnp.zeros((MLA_Q_LORA, n), F32)
    cols = []
    for h in range(4):
        nope = mla_w_uq[l][:, h * dq:h * dq + MLA_NOPE]
        rope = mla_w_uq[l][:, h * dq + MLA_NOPE:(h + 1) * dq]
        cols += ([nope, zq(MLA_NOPE)] if h % 2 == 0 else [zq(MLA_NOPE), nope]) + [rope, zq(LANES - MLA_ROPE)]
    w_uq = jnp.concatenate(cols, axis=1)
    w_uq = jnp.concatenate([w_uq, jnp.zeros((2 * LANES - MLA_Q_LORA, w_uq.shape[1]), F32)], axis=0).astype(BF16)

    dkv = MLA_NOPE + HEAD_DIM
    w_ukv = jnp.concatenate([mla_w_ukv[l][:, h * dkv:h * dkv + MLA_NOPE] for h in range(4)]
                            + [mla_w_ukv[l][:, h * dkv + MLA_NOPE:(h + 1) * dkv] for h in range(4)],
                            axis=1).astype(BF16)

    row = lambda v: v.reshape(1, -1)
    return {
        "w_in_e": w_in_e,
        "gq": row(jnp.tile(qk_norm_q[l], 2)), "gk": row(jnp.tile(qk_norm_k[l], 2)),
        "sgu_g": row(sgu_ln_g[l]), "sgu_b": row(sgu_ln_b[l]),
        "sgu_w": sgu_w[l].reshape(SGU_GROUPS * CHUNK, CHUNK).astype(BF16),
        "sgu_bias": jnp.repeat(sgu_b[l].T, HEAD_DIM, axis=1),
        "gcq": row(jnp.concatenate([mla_q_norm[l], jnp.zeros((2 * LANES - MLA_Q_LORA,), F32)])),
        "w_uq": w_uq, "gckv": row(mla_kv_norm[l]), "w_ukv": w_ukv,
        "conv_a_w": conv_a_w[l], "conv_a_b": row(conv_a_b[l]), "ln_a_g": row(ln_a_g[l]), "ln_a_b": row(ln_a_b[l]),
        "w_out": w_out[l].astype(BF16), "ln_mix_g": row(ln_mix_g[l]), "ln_mix_b": row(ln_mix_b[l]),
        "ffn_w_up": ffn_w_up[l].astype(BF16), "ffn_conv_w": ffn_conv_w[l], "ffn_conv_b": row(ffn_conv_b[l]),
        "ffn_w_down": ffn_w_down[l].astype(BF16), "ln_ffn_g": row(ln_ffn_g[l]), "ln_ffn_b": row(ln_ffn_b[l]),
    }


def kernel(x, ln_in_g, ln_in_b, w_in, conv_a_w, conv_a_b, ln_a_g, ln_a_b, qk_norm_q, qk_norm_k, sgu_ln_g, sgu_ln_b, sgu_w, sgu_b, mla_q_norm, mla_w_uq, mla_kv_norm, mla_w_ukv, w_out, ln_mix_g, ln_mix_b, ffn_w_up, ffn_conv_w, ffn_conv_b, ffn_w_down, ln_ffn_g, ln_ffn_b):
    bsz, seq, d = x.shape
    assert d == D_MODEL and seq % TOKEN_TILE == 0 and seq % ATT_KEY_CHUNK == 0 and seq % GRID_W == 0
    t = bsz * seq
    tabs = _rope_tables(seq)
    h = _input_ln(x.reshape(t, d), ln_in_g.reshape(1, d), ln_in_b.reshape(1, d))
    for l in range(DEPTH):
        lw = _layer_weights(l, w_in, conv_a_w, conv_a_b, ln_a_g, ln_a_b, qk_norm_q, qk_norm_k, sgu_ln_g,
                            sgu_ln_b, sgu_w, sgu_b, mla_q_norm, mla_w_uq, mla_kv_norm, mla_w_ukv, w_out,
                            ln_mix_g, ln_mix_b, ffn_w_up, ffn_conv_w, ffn_conv_b, ffn_w_down, ln_ffn_g, ln_ffn_b)
        a, q_b, k_b, v_b, o_c, q_d, k_d, v_d = _inproj(h, seq, lw["w_in_e"], tabs, lw)
        sh = lambda arr: arr.reshape(bsz, seq, arr.shape[-1])
        o_b = _attention(sh(q_b), sh(k_b), sh(v_b), k_shared=True).reshape(t, GROUP_W)
        o_d = _attention(sh(q_d), sh(k_d), sh(v_d), k_shared=False).reshape(t, GROUP_W)
        h = _mixout(a, o_b, o_c, o_d, h, seq, lw)
        h = _ffn(h, seq, lw)
    return h.reshape(bsz, seq, d)
```

```python
from functools import partial

import jax
import jax.numpy as jnp
from jax import lax
from jax.experimental import pallas as pl
from jax.experimental.pallas import tpu as pltpu

F32 = jnp.float32
BF16 = jnp.bfloat16

D_MODEL = 1024
GRID_W = 64
GROUP_W = 256
HEAD_DIM = 64
CONV_A_WIDTH = 31
CHUNK = 128
SGU_GROUPS = 4
MLA_Q_LORA = 192
MLA_KV_LORA = 128
MLA_NOPE = 64
MLA_ROPE = 32
D_FF = 2816
DEPTH = 2
ROPE_THETA = 10000.0
DEEPNORM_ALPHA = (2 * DEPTH) ** 0.25
LN_EPS = 1e-5
RMS_EPS = 1e-6
LOG2_E = 1.4426950408889634

LANES = 128
F32_SUBLANES = 8
BF16_SUBLANES = 16
VMEM_LIMIT_BYTES = 52 * 1024 * 1024

TOKEN_TILE = 512
ATT_Q_TILE = 256
ATT_KEY_CHUNK = 512
FF_CHUNK = 256
FFN_LOOKAHEAD = 2
CONV_HALO = 16
FFN_HALO = F32_SUBLANES

COL_A = 0
COL_Q = 512
COL_K = 1024
COL_V = 1152
COL_C = 1408
COL_CQ = 1920
COL_CKV = 2176
COL_KR = 2304
IN_COLS = 2432


def _dot(a, b):
    return jnp.dot(a, b, preferred_element_type=F32)


def _layer_norm(x, g, b):
    mu = jnp.mean(x, axis=-1, keepdims=True)
    xc = x - mu
    var = jnp.mean(xc * xc, axis=-1, keepdims=True)
    return xc * lax.rsqrt(var + LN_EPS) * g + b


def _rope(x, cos, sin_signed, half):
    lane = lax.broadcasted_iota(jnp.int32, x.shape, 1)
    first = (lane & (2 * half - 1)) < half
    partner = jnp.where(first, pltpu.roll(x, LANES - half, 1), pltpu.roll(x, half, 1))
    return x * cos + partner * sin_signed


def _ln_kernel(x_ref, g_ref, b_ref, o_ref):
    o_ref[...] = _layer_norm(x_ref[...], g_ref[...], b_ref[...])


def _input_ln(x, g, b):
    t = x.shape[0]
    return pl.pallas_call(
        _ln_kernel,
        grid=(t // TOKEN_TILE,),
        in_specs=[pl.BlockSpec((TOKEN_TILE, D_MODEL), lambda i: (i, 0)),
                  pl.BlockSpec((1, D_MODEL), lambda i: (0, 0)),
                  pl.BlockSpec((1, D_MODEL), lambda i: (0, 0))],
        out_specs=pl.BlockSpec((TOKEN_TILE, D_MODEL), lambda i: (i, 0)),
        out_shape=jax.ShapeDtypeStruct((t, D_MODEL), F32),
        compiler_params=pltpu.CompilerParams(dimension_semantics=("parallel",)),
        name="input_ln",
    )(x, g, b)


def _inproj_kernel(h_ref, w_ref, cos_g_ref, sin_g_ref, cos_m_ref, sin_m_ref,
                   gq_ref, gk_ref, sgu_g_ref, sgu_b_ref, sgu_w_ref, sgu_bias_ref,
                   gcq_ref, w_uq_ref, gckv_ref, w_ukv_ref,
                   a_ref, qb_ref, kb_ref, vb_ref, oc_ref, qd_ref, kd_ref, vd_ref):
    tm = h_ref.shape[0]
    hb = h_ref[...].astype(BF16)

    def proj(col, width):
        return _dot(hb, w_ref[:, col:col + width])

    pc = proj(COL_C, 2 * GROUP_W)
    pcq = proj(COL_CQ, 2 * LANES)
    pckv = proj(COL_CKV, LANES)
    pq = proj(COL_Q, 4 * LANES)
    pk = proj(COL_K, LANES)
    pkr = proj(COL_KR, LANES)

    pc = jax.nn.gelu(pc, approximate=True)
    u = pc[:, :GROUP_W]
    sv = _layer_norm(pc[:, GROUP_W:], sgu_g_ref[...], sgu_b_ref[...]).astype(BF16)
    gates = [_dot(sgu_w_ref[...], sv[n * CHUNK:(n + 1) * CHUNK, :]) for n in range(tm // CHUNK)]

    ms = jnp.sum(pcq * pcq, axis=-1, keepdims=True) * (1.0 / MLA_Q_LORA)
    cqn = (pcq * lax.rsqrt(ms + RMS_EPS) * gcq_ref[...]).astype(BF16)
    qds = [_dot(cqn, w_uq_ref[:, hd * 2 * LANES:(hd + 1) * 2 * LANES]) for hd in range(4)]
    ms = jnp.mean(pckv * pckv, axis=-1, keepdims=True)
    ckvn = (pckv * lax.rsqrt(ms + RMS_EPS) * gckv_ref[...]).astype(BF16)
    kvd = _dot(ckvn, w_ukv_ref[...])

    pa = proj(COL_A, 2 * GROUP_W)
    pv = proj(COL_V, 2 * LANES)

    a_ref[...] = pa[:, :GROUP_W] * jax.nn.sigmoid(pa[:, GROUP_W:])

    cos_g, sin_g = cos_g_ref[...], sin_g_ref[...]
    cos_m, sin_m = cos_m_ref[...], sin_m_ref[...]

    for hd in range(4):
        x = pq[:, hd * LANES:(hd + 1) * LANES]
        ms = jnp.sum(x * x, axis=-1, keepdims=True) * (1.0 / HEAD_DIM)
        xn = x * lax.rsqrt(ms + RMS_EPS) * gq_ref[...]
        xr = _rope(xn, cos_g, sin_g, HEAD_DIM // 4) * (HEAD_DIM ** -0.5 * LOG2_E)
        qb_ref[:, hd * LANES:(hd + 1) * LANES] = xr.astype(BF16)

    low = lax.broadcasted_iota(jnp.int32, pk.shape, 1) < HEAD_DIM
    sq = pk * pk
    ms_lo = jnp.sum(jnp.where(low, sq, 0.0), axis=-1, keepdims=True) * (1.0 / HEAD_DIM)
    ms_hi = jnp.sum(jnp.where(low, 0.0, sq), axis=-1, keepdims=True) * (1.0 / HEAD_DIM)
    rs = jnp.where(low, lax.rsqrt(ms_lo + RMS_EPS), lax.rsqrt(ms_hi + RMS_EPS))
    kb_ref[...] = _rope(pk * rs * gk_ref[...], cos_g, sin_g, HEAD_DIM // 4).astype(BF16)
    vb_ref[...] = pv.astype(BF16)

    lane = lax.broadcasted_iota(jnp.int32, (CHUNK, GROUP_W), 1)
    for n, r in enumerate(gates):
        rows = slice(n * CHUNK, (n + 1) * CHUNK)
        gate = r[3 * CHUNK:4 * CHUNK, :]
        for g in (2, 1, 0):
            gate = jnp.where(lane < (g + 1) * HEAD_DIM, r[g * CHUNK:(g + 1) * CHUNK, :], gate)
        oc_ref[rows, :] = (u[rows, :] * (gate + sgu_bias_ref[...])).astype(BF16)

    scale_d = (MLA_NOPE + MLA_ROPE) ** -0.5 * LOG2_E
    for hd, qd in enumerate(qds):
        qd_ref[:, hd * 2 * LANES:hd * 2 * LANES + LANES] = (qd[:, :LANES] * scale_d).astype(BF16)
        qr = _rope(qd[:, LANES:], cos_m, sin_m, MLA_ROPE // 4) * scale_d
        qd_ref[:, hd * 2 * LANES + LANES:(hd + 1) * 2 * LANES] = qr.astype(BF16)
    kr = _rope(pkr, cos_m, sin_m, MLA_ROPE // 4).astype(BF16)
    for pr in range(2):
        kd_ref[:, pr * 2 * LANES:pr * 2 * LANES + LANES] = kvd[:, pr * LANES:(pr + 1) * LANES].astype(BF16)
        kd_ref[:, pr * 2 * LANES + LANES:(pr + 1) * 2 * LANES] = kr
    vd_ref[...] = kvd[:, 2 * LANES:].astype(BF16)


def _inproj(h, seq, w_in_e, tabs, lw):
    t = h.shape[0]
    tm = TOKEN_TILE
    spt = seq // tm
    tok = lambda width: pl.BlockSpec((tm, width), lambda i: (i, 0))
    pos = pl.BlockSpec((tm, LANES), lambda i: (i % spt, 0))
    full = lambda arr: pl.BlockSpec(arr.shape, lambda i: (0,) * arr.ndim)
    consts = [lw["gq"], lw["gk"], lw["sgu_g"], lw["sgu_b"], lw["sgu_w"], lw["sgu_bias"],
              lw["gcq"], lw["w_uq"], lw["gckv"], lw["w_ukv"]]
    out_widths = [(GROUP_W, F32), (4 * LANES, BF16), (LANES, BF16), (2 * LANES, BF16), (GROUP_W, BF16),
                  (8 * LANES, BF16), (4 * LANES, BF16), (GROUP_W, BF16)]
    return pl.pallas_call(
        _inproj_kernel,
        grid=(t // tm,),
        in_specs=[tok(D_MODEL), full(w_in_e), pos, pos, pos, pos] + [full(c) for c in consts],
        out_specs=[tok(w) for w, _ in out_widths],
        out_shape=[jax.ShapeDtypeStruct((t, w), dt) for w, dt in out_widths],
        compiler_params=pltpu.CompilerParams(dimension_semantics=("parallel",),
                                             vmem_limit_bytes=VMEM_LIMIT_BYTES),
        name="inproj",
    )(h, w_in_e, tabs["cos_g"], tabs["sin_g"], tabs["cos_m"], tabs["sin_m"], *consts)


def _attention_kernel(q_ref, k_ref, v_ref, o_ref):
    tq = q_ref.shape[1]
    kw = k_ref.shape[2]
    seq = k_ref.shape[1]
    q2 = jnp.concatenate([q_ref[0, :, :kw], q_ref[0, :, kw:]], axis=0)
    nchunk = seq // ATT_KEY_CHUNK
    groups = ATT_KEY_CHUNK // LANES

    def scores(c):
        keys = k_ref[0, c * ATT_KEY_CHUNK:(c + 1) * ATT_KEY_CHUNK, :]
        return lax.dot_general(q2, keys, (((1,), (1,)), ((), ())), preferred_element_type=F32)

    s_next = scores(0)
    m = l = acc = None
    for c in range(nchunk):
        s = s_next
        if c + 1 < nchunk:
            s_next = scores(c + 1)
        mc = s[:, :LANES]
        for g in range(1, groups):
            mc = jnp.maximum(mc, s[:, g * LANES:(g + 1) * LANES])
        mc = jnp.max(mc, axis=-1, keepdims=True)
        m_new = mc if c == 0 else jnp.maximum(m, mc)
        p = jnp.exp2(s - m_new)
        ps = p[:, :LANES]
        for g in range(1, groups):
            ps = ps + p[:, g * LANES:(g + 1) * LANES]
        pv = _dot(p.astype(BF16), v_ref[0, c * ATT_KEY_CHUNK:(c + 1) * ATT_KEY_CHUNK, :])
        if c == 0:
            l, acc = ps, pv
        else:
            alpha = jnp.exp2(m - m_new)
            l = alpha * l + ps
            acc = alpha * acc + pv
        m = m_new
    out = acc / jnp.sum(l, axis=-1, keepdims=True)
    lane = lax.broadcasted_iota(jnp.int32, (tq, LANES), 1)
    o_ref[0] = jnp.where(lane < HEAD_DIM, out[:tq], out[tq:]).astype(o_ref.dtype)


def _attention(q, k, v, k_shared):
    bsz, seq, _ = q.shape
    kw = k.shape[2] if k_shared else k.shape[2] // 2
    tq = ATT_Q_TILE
    k_map = (lambda b, p, i: (b, 0, 0)) if k_shared else (lambda b, p, i: (b, 0, p))
    return pl.pallas_call(
        _attention_kernel,
        grid=(bsz, 2, seq // tq),
        in_specs=[pl.BlockSpec((1, tq, 2 * kw), lambda b, p, i: (b, i, p)),
                  pl.BlockSpec((1, seq, kw), k_map),
                  pl.BlockSpec((1, seq, LANES), lambda b, p, i: (b, 0, p))],
        out_specs=pl.BlockSpec((1, tq, LANES), lambda b, p, i: (b, i, p)),
        out_shape=jax.ShapeDtypeStruct((bsz, seq, 2 * LANES), BF16),
        compiler_params=pltpu.CompilerParams(dimension_semantics=("parallel", "parallel", "parallel"),
                                             vmem_limit_bytes=VMEM_LIMIT_BYTES),
        name="attention_shared_k" if k_shared else "attention",
    )(q, k, v)


def _mixout_kernel(a_ref, a_prev_ref, a_next_ref, cw_ref, cb_ref, ga_ref, ba_ref,
                   ob_ref, oc_ref, od_ref, w_ref, h_ref, g_ref, b_ref, o_ref, ext_ref, shift_ref,
                   *, tiles_per_seq):
    tm = a_ref.shape[0]
    i = pl.program_id(0) % tiles_per_seq
    has_prev = jnp.where(i > 0, 1.0, 0.0).astype(F32)
    has_next = jnp.where(i < tiles_per_seq - 1, 1.0, 0.0).astype(F32)
    ext_ref[0:CONV_HALO, :] = a_prev_ref[...] * has_prev
    ext_ref[CONV_HALO:CONV_HALO + tm, :] = a_ref[...]
    ext_ref[CONV_HALO + tm:, :] = a_next_ref[...] * has_next
    half = CONV_A_WIDTH // 2
    span = tm + 2 * CONV_HALO - F32_SUBLANES
    acc = None
    for r in range(F32_SUBLANES):
        shift_ref[r] = ext_ref[r:r + span, :]
        for k in range(CONV_A_WIDTH):
            start = CONV_HALO - half + k
            if start % F32_SUBLANES == r:
                term = shift_ref[r, start - r:start - r + tm, :] * cw_ref[k:k + 1, :]
                acc = term if acc is None else acc + term
    conv = acc + cb_ref[...]
    o_a = jax.nn.silu(_layer_norm(conv, ga_ref[...], ba_ref[...])).astype(BF16)

    mix = _dot(o_a, w_ref[0:GROUP_W, :])
    mix = mix + _dot(ob_ref[...], w_ref[GROUP_W:2 * GROUP_W, :])
    mix = mix + _dot(oc_ref[...], w_ref[2 * GROUP_W:3 * GROUP_W, :])
    mix = mix + _dot(od_ref[...], w_ref[3 * GROUP_W:, :])
    o_ref[...] = _layer_norm(DEEPNORM_ALPHA * h_ref[...] + mix, g_ref[...], b_ref[...])


def _mixout(a, o_b, o_c, o_d, h, seq, lw):
    t = h.shape[0]
    tm = TOKEN_TILE
    spt = seq // tm
    hb = tm // CONV_HALO
    last = t // CONV_HALO - 1
    tok = lambda width: pl.BlockSpec((tm, width), lambda i: (i, 0))
    full = lambda arr: pl.BlockSpec(arr.shape, lambda i: (0,) * arr.ndim)
    consts1 = [lw["conv_a_w"], lw["conv_a_b"], lw["ln_a_g"], lw["ln_a_b"]]
    consts2 = [lw["ln_mix_g"], lw["ln_mix_b"]]
    return pl.pallas_call(
        partial(_mixout_kernel, tiles_per_seq=spt),
        grid=(t // tm,),
        in_specs=[tok(GROUP_W),
                  pl.BlockSpec((CONV_HALO, GROUP_W), lambda i: (jnp.maximum(i * hb - 1, 0), 0)),
                  pl.BlockSpec((CONV_HALO, GROUP_W), lambda i: (jnp.minimum((i + 1) * hb, last), 0))]
                 + [full(c) for c in consts1]
                 + [tok(GROUP_W), tok(GROUP_W), tok(GROUP_W), full(lw["w_out"]), tok(D_MODEL)]
                 + [full(c) for c in consts2],
        out_specs=tok(D_MODEL),
        out_shape=jax.ShapeDtypeStruct((t, D_MODEL), F32),
        scratch_shapes=[pltpu.VMEM((tm + 2 * CONV_HALO, GROUP_W), F32),
                        pltpu.VMEM((F32_SUBLANES, tm + 2 * CONV_HALO - F32_SUBLANES, GROUP_W), F32)],
        compiler_params=pltpu.CompilerParams(dimension_semantics=("parallel",),
                                             vmem_limit_bytes=VMEM_LIMIT_BYTES),
        name="mixout",
    )(a, a, a, *consts1, o_b, o_c, o_d, lw["w_out"], h, *consts2)


def _ffn_kernel(h_ref, h_prev_ref, h_next_ref, w_up_ref, cw_ref, cb_ref, w_down_ref, g_ref, b_ref,
                o_ref, ext_ref, acc_ref, *, tiles_per_seq):
    tm = h_ref.shape[0]
    i = pl.program_id(0) % tiles_per_seq
    has_prev = jnp.where(i > 0, 1.0, 0.0).astype(F32)
    has_next = jnp.where(i < tiles_per_seq - 1, 1.0, 0.0).astype(F32)
    ext_ref[0:tm, :] = h_ref[...].astype(BF16)
    ext_ref[tm:, :] = jnp.concatenate([h_prev_ref[...] * has_prev, h_next_ref[...] * has_next],
                                      axis=0).astype(BF16)
    hext = ext_ref[...]
    row8 = lax.broadcasted_iota(jnp.int32, (F32_SUBLANES, FF_CHUNK), 0)

    def up_proj(col):
        return _dot(hext, w_up_ref[:, col:col + FF_CHUNK])

    def conv(up, col):
        mid = up[:tm]
        before = tm + F32_SUBLANES - 1
        after = tm + F32_SUBLANES
        dn = pltpu.roll(mid, 1, 0)
        dn = jnp.concatenate([jnp.where(row8 == 0, up[before:before + 1], dn[:F32_SUBLANES]),
                              dn[F32_SUBLANES:]], axis=0)
        nx = pltpu.roll(mid, tm - 1, 0)
        nx = jnp.concatenate([nx[:tm - F32_SUBLANES],
                              jnp.where(row8 == F32_SUBLANES - 1, up[after:after + 1], nx[tm - F32_SUBLANES:])],
                             axis=0)
        w = cw_ref[:, col:col + FF_CHUNK]
        return dn * w[0:1] + mid * w[1:2] + nx * w[2:3] + cb_ref[:, col:col + FF_CHUNK]

    nchunk = D_FF // FF_CHUNK
    ups = [(up_proj(j * FF_CHUNK), up_proj(D_FF + j * FF_CHUNK)) for j in range(FFN_LOOKAHEAD)]
    for j in range(nchunk):
        col = j * FF_CHUNK
        if j + FFN_LOOKAHEAD < nchunk:
            ahead = col + FFN_LOOKAHEAD * FF_CHUNK
            ups.append((up_proj(ahead), up_proj(D_FF + ahead)))
        cur = ups[j]
        act = (jax.nn.silu(conv(cur[0], col)) * conv(cur[1], D_FF + col)).astype(BF16)
        part = _dot(act, w_down_ref[col:col + FF_CHUNK, :])
        if j == 0:
            acc_ref[...] = part
        else:
            acc_ref[...] += part
    o_ref[...] = _layer_norm(DEEPNORM_ALPHA * h_ref[...] + acc_ref[...], g_ref[...], b_ref[...])


def _ffn(h, seq, lw):
    t = h.shape[0]
    tm = TOKEN_TILE
    spt = seq // tm
    hb = tm // FFN_HALO
    last = t // FFN_HALO - 1
    tok = pl.BlockSpec((tm, D_MODEL), lambda i: (i, 0))
    full = lambda arr: pl.BlockSpec(arr.shape, lambda i: (0,) * arr.ndim)
    resident = lambda arr: pl.BlockSpec(arr.shape, lambda i: (0,) * arr.ndim, pipeline_mode=pl.Buffered(1))
    return pl.pallas_call(
        partial(_ffn_kernel, tiles_per_seq=spt),
        grid=(t // tm,),
        in_specs=[tok,
                  pl.BlockSpec((FFN_HALO, D_MODEL), lambda i: (jnp.maximum(i * hb - 1, 0), 0)),
                  pl.BlockSpec((FFN_HALO, D_MODEL), lambda i: (jnp.minimum((i + 1) * hb, last), 0)),
                  resident(lw["ffn_w_up"]), full(lw["ffn_conv_w"]), full(lw["ffn_conv_b"]),
                  resident(lw["ffn_w_down"]), full(lw["ln_ffn_g"]), full(lw["ln_ffn_b"])],
        out_specs=tok,
        out_shape=jax.ShapeDtypeStruct((t, D_MODEL), F32),
        scratch_shapes=[pltpu.VMEM((tm + 2 * FFN_HALO, D_MODEL), BF16),
                        pltpu.VMEM((tm, D_MODEL), F32)],
        compiler_params=pltpu.CompilerParams(dimension_semantics=("parallel",),
                                             vmem_limit_bytes=VMEM_LIMIT_BYTES),
        name="ffn",
    )(h, h, h, lw["ffn_w_up"], lw["ffn_conv_w"], lw["ffn_conv_b"], lw["ffn_w_down"],
      lw["ln_ffn_g"], lw["ln_ffn_b"])


def _rope_tables(seq):
    rows = seq // GRID_W
    row = jnp.repeat(jnp.arange(rows, dtype=F32), GRID_W)
    col = jnp.tile(jnp.arange(GRID_W, dtype=F32), rows)

    def tables(half):
        inv_freq = ROPE_THETA ** (-jnp.arange(half, dtype=F32) / half)
        ar = row[:, None] * inv_freq[None, :]
        ac = col[:, None] * inv_freq[None, :]
        cos = jnp.concatenate([jnp.cos(ar), jnp.cos(ar), jnp.cos(ac), jnp.cos(ac)], axis=-1)
        sin = jnp.concatenate([-jnp.sin(ar), jnp.sin(ar), -jnp.sin(ac), jnp.sin(ac)], axis=-1)
        return cos, sin

    cos64, sin64 = tables(HEAD_DIM // 4)
    cos32, sin32 = tables(MLA_ROPE // 4)
    pad = LANES - MLA_ROPE
    return {
        "cos_g": jnp.tile(cos64, (1, 2)), "sin_g": jnp.tile(sin64, (1, 2)),
        "cos_m": jnp.concatenate([cos32, jnp.ones((seq, pad), F32)], axis=-1),
        "sin_m": jnp.concatenate([sin32, jnp.zeros((seq, pad), F32)], axis=-1),
    }


def _layer_weights(l, w_in, conv_a_w, conv_a_b, ln_a_g, ln_a_b, qk_norm_q, qk_norm_k, sgu_ln_g, sgu_ln_b,
                   sgu_w, sgu_b, mla_q_norm, mla_w_uq, mla_kv_norm, mla_w_ukv, w_out, ln_mix_g, ln_mix_b,
                   ffn_w_up, ffn_conv_w, ffn_conv_b, ffn_w_down, ln_ffn_g, ln_ffn_b):
    w = w_in[l]
    z = lambda n: jnp.zeros((D_MODEL, n), F32)
    o = 2 * GROUP_W
    wq = w[:, o:o + 256]
    wk = w[:, o + 256:o + 384]
    wv = w[:, o + 384:o + 512]
    o += 512
    wc = w[:, o:o + 512]
    o += 512
    wcq = w[:, o:o + MLA_Q_LORA]
    wckv = w[:, o + MLA_Q_LORA:o + MLA_Q_LORA + MLA_KV_LORA]
    wkr = w[:, o + MLA_Q_LORA + MLA_KV_LORA:]
    hd = HEAD_DIM
    w_in_e = jnp.concatenate([
        w[:, :2 * GROUP_W],
        wq[:, 0:hd], z(hd), wq[:, hd:2 * hd], z(hd), z(hd), wq[:, 2 * hd:3 * hd], z(hd), wq[:, 3 * hd:],
        wk,
        wv[:, :hd], wv[:, :hd], wv[:, hd:], wv[:, hd:],
        wc,
        wcq, z(2 * LANES - MLA_Q_LORA),
        wckv,
        wkr, z(LANES - MLA_ROPE)], axis=1).astype(BF16)
    assert w_in_e.shape == (D_MODEL, IN_COLS)

    dq = MLA_NOPE + MLA_ROPE
    zq = lambda n: jnp.zeros((MLA_Q_LORA, n), F32)
    cols = []
    for h in range(4):
        nope = mla_w_uq[l][:, h * dq:h * dq + MLA_NOPE]
        rope = mla_w_uq[l][:, h * dq + MLA_NOPE:(h + 1) * dq]
        cols += ([nope, zq(MLA_NOPE)] if h % 2 == 0 else [zq(MLA_NOPE), nope]) + [rope, zq(LANES - MLA_ROPE)]
    w_uq = jnp.concatenate(cols, axis=1)
    w_uq = jnp.concatenate([w_uq, jnp.zeros((2 * LANES - MLA_Q_LORA, w_uq.shape[1]), F32)], axis=0).astype(BF16)

    dkv = MLA_NOPE + HEAD_DIM
    w_ukv = jnp.concatenate([mla_w_ukv[l][:, h * dkv:h * dkv + MLA_NOPE] for h in range(4)]
                            + [mla_w_ukv[l][:, h * dkv + MLA_NOPE:(h + 1) * dkv] for h in range(4)],
                            axis=1).astype(BF16)

    row = lambda v: v.reshape(1, -1)
    return {
        "w_in_e": w_in_e,
        "gq": row(jnp.tile(qk_norm_q[l], 2)), "gk": row(jnp.tile(qk_norm_k[l], 2)),
        "sgu_g": row(sgu_ln_g[l]), "sgu_b": row(sgu_ln_b[l]),
        "sgu_w": sgu_w[l].reshape(SGU_GROUPS * CHUNK, CHUNK).astype(BF16),
        "sgu_bias": jnp.repeat(sgu_b[l].T, HEAD_DIM, axis=1),
        "gcq": row(jnp.concatenate([mla_q_norm[l], jnp.zeros((2 * LANES - MLA_Q_LORA,), F32)])),
        "w_uq": w_uq, "gckv": row(mla_kv_norm[l]), "w_ukv": w_ukv,
        "conv_a_w": conv_a_w[l], "conv_a_b": row(conv_a_b[l]), "ln_a_g": row(ln_a_g[l]), "ln_a_b": row(ln_a_b[l]),
        "w_out": w_out[l].astype(BF16), "ln_mix_g": row(ln_mix_g[l]), "ln_mix_b": row(ln_mix_b[l]),
        "ffn_w_up": ffn_w_up[l].astype(BF16), "ffn_conv_w": ffn_conv_w[l], "ffn_conv_b": row(ffn_conv_b[l]),
        "ffn_w_down": ffn_w_down[l].astype(BF16), "ln_ffn_g": row(ln_ffn_g[l]), "ln_ffn_b": row(ln_ffn_b[l]),
    }


def kernel(x, ln_in_g, ln_in_b, w_in, conv_a_w, conv_a_b, ln_a_g, ln_a_b, qk_norm_q, qk_norm_k, sgu_ln_g, sgu_ln_b, sgu_w, sgu_b, mla_q_norm, mla_w_uq, mla_kv_norm, mla_w_ukv, w_out, ln_mix_g, ln_mix_b, ffn_w_up, ffn_conv_w, ffn_conv_b, ffn_w_down, ln_ffn_g, ln_ffn_b):
    bsz, seq, d = x.shape
    assert d == D_MODEL and seq % TOKEN_TILE == 0 and seq % ATT_KEY_CHUNK == 0 and seq % GRID_W == 0
    t = bsz * seq
    tabs = _rope_tables(seq)
    h = _input_ln(x.reshape(t, d), ln_in_g.reshape(1, d), ln_in_b.reshape(1, d))
    for l in range(DEPTH):
        lw = _layer_weights(l, w_in, conv_a_w, conv_a_b, ln_a_g, ln_a_b, qk_norm_q, qk_norm_k, sgu_ln_g,
                            sgu_ln_b, sgu_w, sgu_b, mla_q_norm, mla_w_uq, mla_kv_norm, mla_w_ukv, w_out,
                            ln_mix_g, ln_mix_b, ffn_w_up, ffn_conv_w, ffn_conv_b, ffn_w_down, ln_ffn_g, ln_ffn_b)
        a, q_b, k_b, v_b, o_c, q_d, k_d, v_d = _inproj(h, seq, lw["w_in_e"], tabs, lw)
        sh = lambda arr: arr.reshape(bsz, seq, arr.shape[-1])
        o_b = _attention(sh(q_b), sh(k_b), sh(v_b), k_shared=True).reshape(t, GROUP_W)
        o_d = _attention(sh(q_d), sh(k_d), sh(v_d), k_shared=False).reshape(t, GROUP_W)
        h = _mixout(a, o_b, o_c, o_d, h, seq, lw)
        h = _ffn(h, seq, lw)
    return h.reshape(bsz, seq, d)
```

```python
from functools import partial

import jax
import jax.numpy as jnp
from jax import lax
from jax.experimental import pallas as pl
from jax.experimental.pallas import tpu as pltpu

F32 = jnp.float32
BF16 = jnp.bfloat16

D_MODEL = 1024
GRID_W = 64
GROUP_W = 256
HEAD_DIM = 64
CONV_A_WIDTH = 31
CHUNK = 128
SGU_GROUPS = 4
MLA_Q_LORA = 192
MLA_KV_LORA = 128
MLA_NOPE = 64
MLA_ROPE = 32
D_FF = 2816
DEPTH = 2
ROPE_THETA = 10000.0
DEEPNORM_ALPHA = (2 * DEPTH) ** 0.25
LN_EPS = 1e-5
RMS_EPS = 1e-6
LOG2_E = 1.4426950408889634

LANES = 128
F32_SUBLANES = 8
BF16_SUBLANES = 16
VMEM_LIMIT_BYTES = 52 * 1024 * 1024

TOKEN_TILE = 512
ATT_Q_TILE = 256
ATT_KEY_CHUNK = 512
ATT_LOOKAHEAD = 2
FF_CHUNK = 256
FFN_LOOKAHEAD = 2
FFN_DOWN_AFTER = (4, 8, 11)
CONV_HALO = 16
FFN_HALO = F32_SUBLANES

COL_A = 0
COL_Q = 512
COL_KV = 768
COL_C = 1024
COL_CQ = 1536
COL_CKV_KR = 1792
IN_COLS = 2048


def _dot(a, b):
    return jnp.dot(a, b, preferred_element_type=F32)


def _layer_norm(x, g, b):
    mu = jnp.mean(x, axis=-1, keepdims=True)
    xc = x - mu
    var = jnp.mean(xc * xc, axis=-1, keepdims=True)
    return xc * lax.rsqrt(var + LN_EPS) * g + b


def _rope(x, cos, sin_signed, half):
    lane = lax.broadcasted_iota(jnp.int32, x.shape, 1)
    first = (lane & (2 * half - 1)) < half
    partner = jnp.where(first, pltpu.roll(x, LANES - half, 1), pltpu.roll(x, half, 1))
    return x * cos + partner * sin_signed


def _ln_kernel(x_ref, g_ref, b_ref, o_ref):
    o_ref[...] = _layer_norm(x_ref[...], g_ref[...], b_ref[...])


def _input_ln(x, g, b):
    t = x.shape[0]
    return pl.pallas_call(
        _ln_kernel,
        grid=(t // TOKEN_TILE,),
        in_specs=[pl.BlockSpec((TOKEN_TILE, D_MODEL), lambda i: (i, 0)),
                  pl.BlockSpec((1, D_MODEL), lambda i: (0, 0)),
                  pl.BlockSpec((1, D_MODEL), lambda i: (0, 0))],
        out_specs=pl.BlockSpec((TOKEN_TILE, D_MODEL), lambda i: (i, 0)),
        out_shape=jax.ShapeDtypeStruct((t, D_MODEL), F32),
        compiler_params=pltpu.CompilerParams(dimension_semantics=("parallel",)),
        name="input_ln",
    )(x, g, b)


def _inproj_kernel(h_ref, w_ref, cos_g_ref, sin_g_ref, cos_m_ref, sin_m_ref,
                   gq_ref, gk_ref, sgu_g_ref, sgu_b_ref, sgu_w_ref, sgu_bias_ref,
                   gcq_ref, w_uq_ref, gckv_ref, w_ukv_ref,
                   a_ref, qb_ref, kb_ref, vb_ref, oc_ref, qd_ref, kd_ref, vd_ref):
    tm = h_ref.shape[0]
    hb = h_ref[...].astype(BF16)

    def proj(col, width):
        return _dot(hb, w_ref[:, col:col + width])

    pc = proj(COL_C, 2 * GROUP_W)
    pcq = proj(COL_CQ, 2 * LANES)
    pckv_kr = proj(COL_CKV_KR, 2 * LANES)
    pckv, pkr = pckv_kr[:, :LANES], pckv_kr[:, LANES:]
    pq = proj(COL_Q, 2 * LANES)
    pkv = proj(COL_KV, 2 * LANES)
    pk, pv = pkv[:, :LANES], pkv[:, LANES:]

    pc = jax.nn.gelu(pc, approximate=True)
    u = pc[:, :GROUP_W]
    sv = _layer_norm(pc[:, GROUP_W:], sgu_g_ref[...], sgu_b_ref[...]).astype(BF16)
    gates = [_dot(sgu_w_ref[...], sv[n * CHUNK:(n + 1) * CHUNK, :]) for n in range(tm // CHUNK)]

    ms = jnp.sum(pcq * pcq, axis=-1, keepdims=True) * (1.0 / MLA_Q_LORA)
    cqn = (pcq * lax.rsqrt(ms + RMS_EPS) * gcq_ref[...]).astype(BF16)
    qds = [_dot(cqn, w_uq_ref[:, hd * 2 * LANES:(hd + 1) * 2 * LANES]) for hd in range(4)]
    ms = jnp.mean(pckv * pckv, axis=-1, keepdims=True)
    ckvn = (pckv * lax.rsqrt(ms + RMS_EPS) * gckv_ref[...]).astype(BF16)
    kvd = _dot(ckvn, w_ukv_ref[...])

    pa = proj(COL_A, 2 * GROUP_W)

    a_ref[...] = pa[:, :GROUP_W] * jax.nn.sigmoid(pa[:, GROUP_W:])

    cos_g, sin_g = cos_g_ref[...], sin_g_ref[...]
    cos_m, sin_m = cos_m_ref[...], sin_m_ref[...]
    low = lax.broadcasted_iota(jnp.int32, (tm, LANES), 1) < HEAD_DIM

    def norm_rope_heads(x, gain):
        sq = x * x
        ms_lo = jnp.sum(jnp.where(low, sq, 0.0), axis=-1, keepdims=True) * (1.0 / HEAD_DIM)
        ms_hi = jnp.sum(jnp.where(low, 0.0, sq), axis=-1, keepdims=True) * (1.0 / HEAD_DIM)
        rs = jnp.where(low, lax.rsqrt(ms_lo + RMS_EPS), lax.rsqrt(ms_hi + RMS_EPS))
        return _rope(x * rs * gain, cos_g, sin_g, HEAD_DIM // 4)

    for kv in range(2):
        xr = norm_rope_heads(pq[:, kv * LANES:(kv + 1) * LANES], gq_ref[...]) * (HEAD_DIM ** -0.5 * LOG2_E)
        swapped = pltpu.roll(xr, HEAD_DIM, 1)
        first, second = (xr, swapped) if kv == 0 else (swapped, xr)
        keep = low if kv == 0 else jnp.logical_not(low)
        qb_ref[:, (2 * kv) * LANES:(2 * kv + 1) * LANES] = jnp.where(keep, first, 0.0).astype(BF16)
        qb_ref[:, (2 * kv + 1) * LANES:(2 * kv + 2) * LANES] = jnp.where(keep, second, 0.0).astype(BF16)

    kb_ref[...] = norm_rope_heads(pk, gk_ref[...]).astype(BF16)
    pv_swapped = pltpu.roll(pv, HEAD_DIM, 1)
    vb_ref[:, :LANES] = jnp.where(low, pv, pv_swapped).astype(BF16)
    vb_ref[:, LANES:] = jnp.where(low, pv_swapped, pv).astype(BF16)

    lane = lax.broadcasted_iota(jnp.int32, (CHUNK, GROUP_W), 1)
    for n, r in enumerate(gates):
        rows = slice(n * CHUNK, (n + 1) * CHUNK)
        gate = r[3 * CHUNK:4 * CHUNK, :]
        for g in (2, 1, 0):
            gate = jnp.where(lane < (g + 1) * HEAD_DIM, r[g * CHUNK:(g + 1) * CHUNK, :], gate)
        oc_ref[rows, :] = (u[rows, :] * (gate + sgu_bias_ref[...])).astype(BF16)

    scale_d = (MLA_NOPE + MLA_ROPE) ** -0.5 * LOG2_E
    for hd, qd in enumerate(qds):
        qd_ref[:, hd * 2 * LANES:hd * 2 * LANES + LANES] = (qd[:, :LANES] * scale_d).astype(BF16)
        qr = _rope(qd[:, LANES:], cos_m, sin_m, MLA_ROPE // 4) * scale_d
        qd_ref[:, hd * 2 * LANES + LANES:(hd + 1) * 2 * LANES] = qr.astype(BF16)
    kr = _rope(pkr, cos_m, sin_m, MLA_ROPE // 4).astype(BF16)
    for pr in range(2):
        kd_ref[:, pr * 2 * LANES:pr * 2 * LANES + LANES] = kvd[:, pr * LANES:(pr + 1) * LANES].astype(BF16)
        kd_ref[:, pr * 2 * LANES + LANES:(pr + 1) * 2 * LANES] = kr
    vd_ref[...] = kvd[:, 2 * LANES:].astype(BF16)


def _inproj(h, seq, w_in_e, tabs, lw):
    t = h.shape[0]
    tm = TOKEN_TILE
    spt = seq // tm
    tok = lambda width: pl.BlockSpec((tm, width), lambda i: (i, 0))
    pos = pl.BlockSpec((tm, LANES), lambda i: (i % spt, 0))
    full = lambda arr: pl.BlockSpec(arr.shape, lambda i: (0,) * arr.ndim)
    consts = [lw["gq"], lw["gk"], lw["sgu_g"], lw["sgu_b"], lw["sgu_w"], lw["sgu_bias"],
              lw["gcq"], lw["w_uq"], lw["gckv"], lw["w_ukv"]]
    out_widths = [(GROUP_W, F32), (4 * LANES, BF16), (LANES, BF16), (2 * LANES, BF16), (GROUP_W, BF16),
                  (8 * LANES, BF16), (4 * LANES, BF16), (GROUP_W, BF16)]
    return pl.pallas_call(
        _inproj_kernel,
        grid=(t // tm,),
        in_specs=[tok(D_MODEL), full(w_in_e), pos, pos, pos, pos] + [full(c) for c in consts],
        out_specs=[tok(w) for w, _ in out_widths],
        out_shape=[jax.ShapeDtypeStruct((t, w), dt) for w, dt in out_widths],
        compiler_params=pltpu.CompilerParams(dimension_semantics=("parallel",),
                                             vmem_limit_bytes=VMEM_LIMIT_BYTES),
        name="inproj",
    )(h, w_in_e, tabs["cos_g"], tabs["sin_g"], tabs["cos_m"], tabs["sin_m"], *consts)


def _attention_kernel(q_ref, k_ref, v_ref, o_ref):
    tq = q_ref.shape[1]
    kw = k_ref.shape[2]
    seq = k_ref.shape[1]
    q2 = jnp.concatenate([q_ref[0, :, :kw], q_ref[0, :, kw:]], axis=0)
    nchunk = seq // ATT_KEY_CHUNK
    groups = ATT_KEY_CHUNK // LANES

    def scores(c):
        keys = k_ref[0, c * ATT_KEY_CHUNK:(c + 1) * ATT_KEY_CHUNK, :]
        return lax.dot_general(q2, keys, (((1,), (1,)), ((), ())), preferred_element_type=F32)

    ss = [scores(c) for c in range(ATT_LOOKAHEAD)]
    m = l = acc = None
    for c in range(nchunk):
        if c + ATT_LOOKAHEAD < nchunk:
            ss.append(scores(c + ATT_LOOKAHEAD))
        s = ss[c]
        mc = s[:, :LANES]
        for g in range(1, groups):
            mc = jnp.maximum(mc, s[:, g * LANES:(g + 1) * LANES])
        mc = jnp.max(mc, axis=-1, keepdims=True)
        m_new = mc if c == 0 else jnp.maximum(m, mc)
        p = jnp.exp2(s - m_new)
        ps = p[:, :LANES]
        for g in range(1, groups):
            ps = ps + p[:, g * LANES:(g + 1) * LANES]
        pv = _dot(p.astype(BF16), v_ref[0, c * ATT_KEY_CHUNK:(c + 1) * ATT_KEY_CHUNK, :])
        if c == 0:
            l, acc = ps, pv
        else:
            alpha = jnp.exp2(m - m_new)
            l = alpha * l + ps
            acc = alpha * acc + pv
        m = m_new
    out = acc / jnp.sum(l, axis=-1, keepdims=True)
    lane = lax.broadcasted_iota(jnp.int32, (tq, LANES), 1)
    o_ref[0] = jnp.where(lane < HEAD_DIM, out[:tq], out[tq:]).astype(o_ref.dtype)


def _attention(q, k, v, k_shared):
    bsz, seq, _ = q.shape
    kw = k.shape[2] if k_shared else k.shape[2] // 2
    tq = ATT_Q_TILE
    k_map = (lambda b, p, i: (b, 0, 0)) if k_shared else (lambda b, p, i: (b, 0, p))
    return pl.pallas_call(
        _attention_kernel,
        grid=(bsz, 2, seq // tq),
        in_specs=[pl.BlockSpec((1, tq, 2 * kw), lambda b, p, i: (b, i, p)),
                  pl.BlockSpec((1, seq, kw), k_map),
                  pl.BlockSpec((1, seq, LANES), lambda b, p, i: (b, 0, p))],
        out_specs=pl.BlockSpec((1, tq, LANES), lambda b, p, i: (b, i, p)),
        out_shape=jax.ShapeDtypeStruct((bsz, seq, 2 * LANES), BF16),
        compiler_params=pltpu.CompilerParams(dimension_semantics=("parallel", "parallel", "parallel"),
                                             vmem_limit_bytes=VMEM_LIMIT_BYTES),
        name="attention_shared_k" if k_shared else "attention",
    )(q, k, v)


def _mixout_kernel(a_ref, a_prev_ref, a_next_ref, cw_ref, cb_ref, ga_ref, ba_ref,
                   ob_ref, oc_ref, od_ref, w_ref, h_ref, g_ref, b_ref, o_ref, ext_ref, shift_ref,
                   *, tiles_per_seq):
    tm = a_ref.shape[0]
    i = pl.program_id(0) % tiles_per_seq
    has_prev = jnp.where(i > 0, 1.0, 0.0).astype(F32)
    has_next = jnp.where(i < tiles_per_seq - 1, 1.0, 0.0).astype(F32)
    ext_ref[0:CONV_HALO, :] = a_prev_ref[...] * has_prev
    ext_ref[CONV_HALO:CONV_HALO + tm, :] = a_ref[...]
    ext_ref[CONV_HALO + tm:, :] = a_next_ref[...] * has_next
    mix = _dot(ob_ref[...], w_ref[GROUP_W:2 * GROUP_W, :])
    mix = mix + _dot(oc_ref[...], w_ref[2 * GROUP_W:3 * GROUP_W, :])
    mix = mix + _dot(od_ref[...], w_ref[3 * GROUP_W:, :])

    half = CONV_A_WIDTH // 2
    span = tm + 2 * CONV_HALO - F32_SUBLANES
    acc = None
    for r in range(F32_SUBLANES):
        shift_ref[r] = ext_ref[r:r + span, :]
        for k in range(CONV_A_WIDTH):
            start = CONV_HALO - half + k
            if start % F32_SUBLANES == r:
                term = shift_ref[r, start - r:start - r + tm, :] * cw_ref[k:k + 1, :]
                acc = term if acc is None else acc + term
    conv = acc + cb_ref[...]
    o_a = jax.nn.silu(_layer_norm(conv, ga_ref[...], ba_ref[...])).astype(BF16)

    mix = mix + _dot(o_a, w_ref[0:GROUP_W, :])
    o_ref[...] = _layer_norm(DEEPNORM_ALPHA * h_ref[...] + mix, g_ref[...], b_ref[...])


def _mixout(a, o_b, o_c, o_d, h, seq, lw):
    t = h.shape[0]
    tm = TOKEN_TILE
    spt = seq // tm
    hb = tm // CONV_HALO
    last = t // CONV_HALO - 1
    tok = lambda width: pl.BlockSpec((tm, width), lambda i: (i, 0))
    full = lambda arr: pl.BlockSpec(arr.shape, lambda i: (0,) * arr.ndim)
    consts1 = [lw["conv_a_w"], lw["conv_a_b"], lw["ln_a_g"], lw["ln_a_b"]]
    consts2 = [lw["ln_mix_g"], lw["ln_mix_b"]]
    return pl.pallas_call(
        partial(_mixout_kernel, tiles_per_seq=spt),
        grid=(t // tm,),
        in_specs=[tok(GROUP_W),
                  pl.BlockSpec((CONV_HALO, GROUP_W), lambda i: (jnp.maximum(i * hb - 1, 0), 0)),
                  pl.BlockSpec((CONV_HALO, GROUP_W), lambda i: (jnp.minimum((i + 1) * hb, last), 0))]
                 + [full(c) for c in consts1]
                 + [tok(GROUP_W), tok(GROUP_W), tok(GROUP_W), full(lw["w_out"]), tok(D_MODEL)]
                 + [full(c) for c in consts2],
        out_specs=tok(D_MODEL),
        out_shape=jax.ShapeDtypeStruct((t, D_MODEL), F32),
        scratch_shapes=[pltpu.VMEM((tm + 2 * CONV_HALO, GROUP_W), F32),
                        pltpu.VMEM((F32_SUBLANES, tm + 2 * CONV_HALO - F32_SUBLANES, GROUP_W), F32)],
        compiler_params=pltpu.CompilerParams(dimension_semantics=("parallel",),
                                             vmem_limit_bytes=VMEM_LIMIT_BYTES),
        name="mixout",
    )(a, a, a, *consts1, o_b, o_c, o_d, lw["w_out"], h, *consts2)


def _ffn_kernel(h_ref, h_prev_ref, h_next_ref, w_up_ref, cw_ref, cb_ref, w_down_ref, g_ref, b_ref,
                o_ref, ext_ref, act_ref, *, tiles_per_seq):
    tm = h_ref.shape[0]
    i = pl.program_id(0) % tiles_per_seq
    has_prev = jnp.where(i > 0, 1.0, 0.0).astype(F32)
    has_next = jnp.where(i < tiles_per_seq - 1, 1.0, 0.0).astype(F32)
    ext_ref[0:tm, :] = h_ref[...].astype(BF16)
    ext_ref[tm:, :] = jnp.concatenate([h_prev_ref[...] * has_prev, h_next_ref[...] * has_next],
                                      axis=0).astype(BF16)
    hext = ext_ref[...]
    row8 = lax.broadcasted_iota(jnp.int32, (F32_SUBLANES, FF_CHUNK), 0)

    def up_proj(col):
        return _dot(hext, w_up_ref[:, col:col + FF_CHUNK])

    def conv(up, col):
        mid = up[:tm]
        before = tm + F32_SUBLANES - 1
        after = tm + F32_SUBLANES
        dn = pltpu.roll(mid, 1, 0)
        dn = jnp.concatenate([jnp.where(row8 == 0, up[before:before + 1], dn[:F32_SUBLANES]),
                              dn[F32_SUBLANES:]], axis=0)
        nx = pltpu.roll(mid, tm - 1, 0)
        nx = jnp.concatenate([nx[:tm - F32_SUBLANES],
                              jnp.where(row8 == F32_SUBLANES - 1, up[after:after + 1], nx[tm - F32_SUBLANES:])],
                             axis=0)
        w = cw_ref[:, col:col + FF_CHUNK]
        return dn * w[0:1] + mid * w[1:2] + nx * w[2:3] + cb_ref[:, col:col + FF_CHUNK]

    nchunk = D_FF // FF_CHUNK
    ups = [(up_proj(j * FF_CHUNK), up_proj(D_FF + j * FF_CHUNK)) for j in range(FFN_LOOKAHEAD)]
    down = None
    done = 0
    for j in range(nchunk):
        col = j * FF_CHUNK
        if j + FFN_LOOKAHEAD < nchunk:
            ahead = col + FFN_LOOKAHEAD * FF_CHUNK
            ups.append((up_proj(ahead), up_proj(D_FF + ahead)))
        cur = ups[j]
        act_ref[:, col:col + FF_CHUNK] = (jax.nn.silu(conv(cur[0], col)) * conv(cur[1], D_FF + col)).astype(BF16)
        if j + 1 in FFN_DOWN_AFTER:
            lo, hi = done * FF_CHUNK, (j + 1) * FF_CHUNK
            part = _dot(act_ref[:, lo:hi], w_down_ref[lo:hi, :])
            down = part if down is None else down + part
            done = j + 1
    o_ref[...] = _layer_norm(DEEPNORM_ALPHA * h_ref[...] + down, g_ref[...], b_ref[...])


def _ffn(h, seq, lw):
    t = h.shape[0]
    tm = TOKEN_TILE
    spt = seq // tm
    hb = tm // FFN_HALO
    last = t // FFN_HALO - 1
    tok = pl.BlockSpec((tm, D_MODEL), lambda i: (i, 0))
    full = lambda arr: pl.BlockSpec(arr.shape, lambda i: (0,) * arr.ndim)
    resident = lambda arr: pl.BlockSpec(arr.shape, lambda i: (0,) * arr.ndim, pipeline_mode=pl.Buffered(1))
    return pl.pallas_call(
        partial(_ffn_kernel, tiles_per_seq=spt),
        grid=(t // tm,),
        in_specs=[tok,
                  pl.BlockSpec((FFN_HALO, D_MODEL), lambda i: (jnp.maximum(i * hb - 1, 0), 0)),
                  pl.BlockSpec((FFN_HALO, D_MODEL), lambda i: (jnp.minimum((i + 1) * hb, last), 0)),
                  resident(lw["ffn_w_up"]), full(lw["ffn_conv_w"]), full(lw["ffn_conv_b"]),
                  resident(lw["ffn_w_down"]), full(lw["ln_ffn_g"]), full(lw["ln_ffn_b"])],
        out_specs=tok,
        out_shape=jax.ShapeDtypeStruct((t, D_MODEL), F32),
        scratch_shapes=[pltpu.VMEM((tm + 2 * FFN_HALO, D_MODEL), BF16),
                        pltpu.VMEM((tm, D_FF), BF16)],
        compiler_params=pltpu.CompilerParams(dimension_semantics=("parallel",),
                                             vmem_limit_bytes=VMEM_LIMIT_BYTES),
        name="ffn",
    )(h, h, h, lw["ffn_w_up"], lw["ffn_conv_w"], lw["ffn_conv_b"], lw["ffn_w_down"],
      lw["ln_ffn_g"], lw["ln_ffn_b"])


def _rope_tables(seq):
    rows = seq // GRID_W
    row = jnp.repeat(jnp.arange(rows, dtype=F32), GRID_W)
    col = jnp.tile(jnp.arange(GRID_W, dtype=F32), rows)

    def tables(half):
        inv_freq = ROPE_THETA ** (-jnp.arange(half, dtype=F32) / half)
        ar = row[:, None] * inv_freq[None, :]
        ac = col[:, None] * inv_freq[None, :]
        cos = jnp.concatenate([jnp.cos(ar), jnp.cos(ar), jnp.cos(ac), jnp.cos(ac)], axis=-1)
        sin = jnp.concatenate([-jnp.sin(ar), jnp.sin(ar), -jnp.sin(ac), jnp.sin(ac)], axis=-1)
        return cos, sin

    cos64, sin64 = tables(HEAD_DIM // 4)
    cos32, sin32 = tables(MLA_ROPE // 4)
    pad = LANES - MLA_ROPE
    return {
        "cos_g": jnp.tile(cos64, (1, 2)), "sin_g": jnp.tile(sin64, (1, 2)),
        "cos_m": jnp.concatenate([cos32, jnp.ones((seq, pad), F32)], axis=-1),
        "sin_m": jnp.concatenate([sin32, jnp.zeros((seq, pad), F32)], axis=-1),
    }


def _layer_weights(l, w_in, conv_a_w, conv_a_b, ln_a_g, ln_a_b, qk_norm_q, qk_norm_k, sgu_ln_g, sgu_ln_b,
                   sgu_w, sgu_b, mla_q_norm, mla_w_uq, mla_kv_norm, mla_w_ukv, w_out, ln_mix_g, ln_mix_b,
                   ffn_w_up, ffn_conv_w, ffn_conv_b, ffn_w_down, ln_ffn_g, ln_ffn_b):
    w = w_in[l]
    z = lambda n: jnp.zeros((D_MODEL, n), F32)
    end_cq = COL_CQ + MLA_Q_LORA
    w_in_e = jnp.concatenate([w[:, :end_cq], z(2 * LANES - MLA_Q_LORA),
                              w[:, end_cq:], z(LANES - MLA_ROPE)], axis=1).astype(BF16)
    assert w.shape[1] == end_cq + MLA_KV_LORA + MLA_ROPE and w_in_e.shape == (D_MODEL, IN_COLS)

    dq = MLA_NOPE + MLA_ROPE
    zq = lambda n: jnp.zeros((MLA_Q_LORA, n), F32)
    cols = []
    for h in range(4):
        nope = mla_w_uq[l][:, h * dq:h * dq + MLA_NOPE]
        rope = mla_w_uq[l][:, h * dq + MLA_NOPE:(h + 1) * dq]
        cols += ([nope, zq(MLA_NOPE)] if h % 2 == 0 else [zq(MLA_NOPE), nope]) + [rope, zq(LANES - MLA_ROPE)]
    w_uq = jnp.concatenate(cols, axis=1)
    w_uq = jnp.concatenate([w_uq, jnp.zeros((2 * LANES - MLA_Q_LORA, w_uq.shape[1]), F32)], axis=0).astype(BF16)

    dkv = MLA_NOPE + HEAD_DIM
    w_ukv = jnp.concatenate([mla_w_ukv[l][:, h * dkv:h * dkv + MLA_NOPE] for h in range(4)]
                            + [mla_w_ukv[l][:, h * dkv + MLA_NOPE:(h + 1) * dkv] for h in range(4)],
                            axis=1).astype(BF16)

    row = lambda v: v.reshape(1, -1)
    return {
        "w_in_e": w_in_e,
        "gq": row(jnp.tile(qk_norm_q[l], 2)), "gk": row(jnp.tile(qk_norm_k[l], 2)),
        "sgu_g": row(sgu_ln_g[l]), "sgu_b": row(sgu_ln_b[l]),
        "sgu_w": sgu_w[l].reshape(SGU_GROUPS * CHUNK, CHUNK).astype(BF16),
        "sgu_bias": jnp.repeat(sgu_b[l].T, HEAD_DIM, axis=1),
        "gcq": row(jnp.concatenate([mla_q_norm[l], jnp.zeros((2 * LANES - MLA_Q_LORA,), F32)])),
        "w_uq": w_uq, "gckv": row(mla_kv_norm[l]), "w_ukv": w_ukv,
        "conv_a_w": conv_a_w[l], "conv_a_b": row(conv_a_b[l]), "ln_a_g": row(ln_a_g[l]), "ln_a_b": row(ln_a_b[l]),
        "w_out": w_out[l].astype(BF16), "ln_mix_g": row(ln_mix_g[l]), "ln_mix_b": row(ln_mix_b[l]),
        "ffn_w_up": ffn_w_up[l].astype(BF16), "ffn_conv_w": ffn_conv_w[l], "ffn_conv_b": row(ffn_conv_b[l]),
        "ffn_w_down": ffn_w_down[l].astype(BF16), "ln_ffn_g": row(ln_ffn_g[l]), "ln_ffn_b": row(ln_ffn_b[l]),
    }


def kernel(x, ln_in_g, ln_in_b, w_in, conv_a_w, conv_a_b, ln_a_g, ln_a_b, qk_norm_q, qk_norm_k, sgu_ln_g, sgu_ln_b, sgu_w, sgu_b, mla_q_norm, mla_w_uq, mla_kv_norm, mla_w_ukv, w_out, ln_mix_g, ln_mix_b, ffn_w_up, ffn_conv_w, ffn_conv_b, ffn_w_down, ln_ffn_g, ln_ffn_b):
    bsz, seq, d = x.shape
    assert d == D_MODEL and seq % TOKEN_TILE == 0 and seq % ATT_KEY_CHUNK == 0 and seq % GRID_W == 0
    t = bsz * seq
    tabs = _rope_tables(seq)
    h = _input_ln(x.reshape(t, d), ln_in_g.reshape(1, d), ln_in_b.reshape(1, d))
    for l in range(DEPTH):
        lw = _layer_weights(l, w_in, conv_a_w, conv_a_b, ln_a_g, ln_a_b, qk_norm_q, qk_norm_k, sgu_ln_g,
                            sgu_ln_b, sgu_w, sgu_b, mla_q_norm, mla_w_uq, mla_kv_norm, mla_w_ukv, w_out,
                            ln_mix_g, ln_mix_b, ffn_w_up, ffn_conv_w, ffn_conv_b, ffn_w_down, ln_ffn_g, ln_ffn_b)
        a, q_b, k_b, v_b, o_c, q_d, k_d, v_d = _inproj(h, seq, lw["w_in_e"], tabs, lw)
        sh = lambda arr: arr.reshape(bsz, seq, arr.shape[-1])
        o_b = _attention(sh(q_b), sh(k_b), sh(v_b), k_shared=True).reshape(t, GROUP_W)
        o_d = _attention(sh(q_d), sh(k_d), sh(v_d), k_shared=False).reshape(t, GROUP_W)
        h = _mixout(a, o_b, o_c, o_d, h, seq, lw)
        h = _ffn(h, seq, lw)
    return h.reshape(bsz, seq, d)
```

```python
from functools import partial

import jax
import jax.numpy as jnp
from jax import lax
from jax.experimental import pallas as pl
from jax.experimental.pallas import tpu as pltpu

F32 = jnp.float32
BF16 = jnp.bfloat16

D_MODEL = 1024
GRID_W = 64
GROUP_W = 256
HEAD_DIM = 64
CONV_A_WIDTH = 31
CHUNK = 128
SGU_GROUPS = 4
MLA_Q_LORA = 192
MLA_KV_LORA = 128
MLA_NOPE = 64
MLA_ROPE = 32
D_FF = 2816
DEPTH = 2
ROPE_THETA = 10000.0
DEEPNORM_ALPHA = (2 * DEPTH) ** 0.25
LN_EPS = 1e-5
RMS_EPS = 1e-6
LOG2_E = 1.4426950408889634

LANES = 128
F32_SUBLANES = 8
BF16_SUBLANES = 16
VMEM_LIMIT_BYTES = 52 * 1024 * 1024

TOKEN_TILE = 512
ATT_Q_TILE = 256
ATT_KEY_CHUNK = 512
ATT_TILES_PER_STEP = 4
ATT_LOOKAHEAD = 2
FF_CHUNK = 256
FFN_LOOKAHEAD = 2
FFN_DOWN_AFTER = (4, 8, 11)
CONV_HALO = 16
FFN_HALO = F32_SUBLANES

COL_A = 0
COL_Q = 512
COL_KV = 768
COL_C = 1024
COL_CQ = 1536
COL_CKV_KR = 1792
IN_COLS = 2048


def _dot(a, b):
    return jnp.dot(a, b, preferred_element_type=F32)


def _layer_norm(x, g, b):
    mu = jnp.mean(x, axis=-1, keepdims=True)
    xc = x - mu
    var = jnp.mean(xc * xc, axis=-1, keepdims=True)
    return xc * lax.rsqrt(var + LN_EPS) * g + b


def _rope(x, cos, sin_signed, half):
    lane = lax.broadcasted_iota(jnp.int32, x.shape, 1)
    first = (lane & (2 * half - 1)) < half
    partner = jnp.where(first, pltpu.roll(x, LANES - half, 1), pltpu.roll(x, half, 1))
    return x * cos + partner * sin_signed


def _inproj_kernel(h_ref, *refs):
    _inproj_body(h_ref[...], *refs)


def _ln_inproj_kernel(x_ref, ln_g_ref, ln_b_ref, *refs):
    h = _layer_norm(x_ref[...], ln_g_ref[...], ln_b_ref[...])
    refs[-1][...] = h
    _inproj_body(h, *refs[:-1])


def _inproj_body(h, w_ref, cos_g_ref, sin_g_ref, cos_m_ref, sin_m_ref,
                 gq_ref, gk_ref, sgu_g_ref, sgu_b_ref, sgu_w_ref, sgu_bias_ref,
                 gcq_ref, w_uq_ref, gckv_ref, w_ukv_ref,
                 a_ref, qb_ref, kb_ref, vb_ref, oc_ref, qd_ref, kd_ref, vd_ref):
    tm = h.shape[0]
    hb = h.astype(BF16)

    def proj(col, width):
        return _dot(hb, w_ref[:, col:col + width])

    pc = proj(COL_C, 2 * GROUP_W)
    pcq = proj(COL_CQ, 2 * LANES)
    pckv_kr = proj(COL_CKV_KR, 2 * LANES)
    pckv, pkr = pckv_kr[:, :LANES], pckv_kr[:, LANES:]
    pq = proj(COL_Q, 2 * LANES)
    pkv = proj(COL_KV, 2 * LANES)
    pk, pv = pkv[:, :LANES], pkv[:, LANES:]

    pc = jax.nn.gelu(pc, approximate=True)
    u = pc[:, :GROUP_W]
    sv = _layer_norm(pc[:, GROUP_W:], sgu_g_ref[...], sgu_b_ref[...]).astype(BF16)
    gates = [_dot(sgu_w_ref[...], sv[n * CHUNK:(n + 1) * CHUNK, :]) for n in range(tm // CHUNK)]

    ms = jnp.sum(pcq * pcq, axis=-1, keepdims=True) * (1.0 / MLA_Q_LORA)
    cqn = (pcq * lax.rsqrt(ms + RMS_EPS) * gcq_ref[...]).astype(BF16)
    qds = [_dot(cqn, w_uq_ref[:, hd * 2 * LANES:(hd + 1) * 2 * LANES]) for hd in range(4)]
    ms = jnp.mean(pckv * pckv, axis=-1, keepdims=True)
    ckvn = (pckv * lax.rsqrt(ms + RMS_EPS) * gckv_ref[...]).astype(BF16)
    kvd = _dot(ckvn, w_ukv_ref[...])

    pa = proj(COL_A, 2 * GROUP_W)

    a_ref[...] = pa[:, :GROUP_W] * jax.nn.sigmoid(pa[:, GROUP_W:])

    cos_g, sin_g = cos_g_ref[...], sin_g_ref[...]
    cos_m, sin_m = cos_m_ref[...], sin_m_ref[...]
    low = lax.broadcasted_iota(jnp.int32, (tm, LANES), 1) < HEAD_DIM

    def norm_rope_heads(x, gain):
        sq = x * x
        ms_lo = jnp.sum(jnp.where(low, sq, 0.0), axis=-1, keepdims=True) * (1.0 / HEAD_DIM)
        ms_hi = jnp.sum(jnp.where(low, 0.0, sq), axis=-1, keepdims=True) * (1.0 / HEAD_DIM)
        rs = jnp.where(low, lax.rsqrt(ms_lo + RMS_EPS), lax.rsqrt(ms_hi + RMS_EPS))
        return _rope(x * rs * gain, cos_g, sin_g, HEAD_DIM // 4)

    for kv in range(2):
        xr = norm_rope_heads(pq[:, kv * LANES:(kv + 1) * LANES], gq_ref[...]) * (HEAD_DIM ** -0.5 * LOG2_E)
        swapped = pltpu.roll(xr, HEAD_DIM, 1)
        first, second = (xr, swapped) if kv == 0 else (swapped, xr)
        keep = low if kv == 0 else jnp.logical_not(low)
        qb_ref[:, (2 * kv) * LANES:(2 * kv + 1) * LANES] = jnp.where(keep, first, 0.0).astype(BF16)
        qb_ref[:, (2 * kv + 1) * LANES:(2 * kv + 2) * LANES] = jnp.where(keep, second, 0.0).astype(BF16)

    kb_ref[...] = norm_rope_heads(pk, gk_ref[...]).astype(BF16)
    pv_swapped = pltpu.roll(pv, HEAD_DIM, 1)
    vb_ref[:, :LANES] = jnp.where(low, pv, pv_swapped).astype(BF16)
    vb_ref[:, LANES:] = jnp.where(low, pv_swapped, pv).astype(BF16)

    lane = lax.broadcasted_iota(jnp.int32, (CHUNK, GROUP_W), 1)
    for n, r in enumerate(gates):
        rows = slice(n * CHUNK, (n + 1) * CHUNK)
        gate = r[3 * CHUNK:4 * CHUNK, :]
        for g in (2, 1, 0):
            gate = jnp.where(lane < (g + 1) * HEAD_DIM, r[g * CHUNK:(g + 1) * CHUNK, :], gate)
        oc_ref[rows, :] = (u[rows, :] * (gate + sgu_bias_ref[...])).astype(BF16)

    scale_d = (MLA_NOPE + MLA_ROPE) ** -0.5 * LOG2_E
    for hd, qd in enumerate(qds):
        qd_ref[:, hd * 2 * LANES:hd * 2 * LANES + LANES] = (qd[:, :LANES] * scale_d).astype(BF16)
        qr = _rope(qd[:, LANES:], cos_m, sin_m, MLA_ROPE // 4) * scale_d
        qd_ref[:, hd * 2 * LANES + LANES:(hd + 1) * 2 * LANES] = qr.astype(BF16)
    kr = _rope(pkr, cos_m, sin_m, MLA_ROPE // 4).astype(BF16)
    for pr in range(2):
        kd_ref[:, pr * 2 * LANES:pr * 2 * LANES + LANES] = kvd[:, pr * LANES:(pr + 1) * LANES].astype(BF16)
        kd_ref[:, pr * 2 * LANES + LANES:(pr + 1) * 2 * LANES] = kr
    vd_ref[...] = kvd[:, 2 * LANES:].astype(BF16)


def _inproj(h, seq, w_in_e, tabs, lw, input_ln=None):
    t = h.shape[0]
    tm = TOKEN_TILE
    spt = seq // tm
    tok = lambda width: pl.BlockSpec((tm, width), lambda i: (i, 0))
    pos = pl.BlockSpec((tm, LANES), lambda i: (i % spt, 0))
    full = lambda arr: pl.BlockSpec(arr.shape, lambda i: (0,) * arr.ndim)
    consts = [lw["gq"], lw["gk"], lw["sgu_g"], lw["sgu_b"], lw["sgu_w"], lw["sgu_bias"],
              lw["gcq"], lw["w_uq"], lw["gckv"], lw["w_ukv"]]
    out_widths = [(GROUP_W, F32), (4 * LANES, BF16), (LANES, BF16), (2 * LANES, BF16), (GROUP_W, BF16),
                  (8 * LANES, BF16), (4 * LANES, BF16), (GROUP_W, BF16)]
    ln_args = [] if input_ln is None else list(input_ln)
    if input_ln is not None:
        out_widths = out_widths + [(D_MODEL, F32)]
    return pl.pallas_call(
        _inproj_kernel if input_ln is None else _ln_inproj_kernel,
        grid=(t // tm,),
        in_specs=[tok(D_MODEL)] + [full(c) for c in ln_args] + [full(w_in_e), pos, pos, pos, pos]
                 + [full(c) for c in consts],
        out_specs=[tok(w) for w, _ in out_widths],
        out_shape=[jax.ShapeDtypeStruct((t, w), dt) for w, dt in out_widths],
        compiler_params=pltpu.CompilerParams(dimension_semantics=("parallel",),
                                             vmem_limit_bytes=VMEM_LIMIT_BYTES),
        name="inproj" if input_ln is None else "ln_inproj",
    )(h, *ln_args, w_in_e, tabs["cos_g"], tabs["sin_g"], tabs["cos_m"], tabs["sin_m"], *consts)


def _attention_kernel(q_ref, k_ref, v_ref, o_ref):
    tq = ATT_Q_TILE
    kw = k_ref.shape[2]
    seq = k_ref.shape[1]
    nchunk = seq // ATT_KEY_CHUNK
    groups = ATT_KEY_CHUNK // LANES
    lane = lax.broadcasted_iota(jnp.int32, (tq, LANES), 1)
    items = [(t, c) for t in range(q_ref.shape[1] // tq) for c in range(nchunk)]

    def scores(item):
        t, c = item
        rows = slice(t * tq, (t + 1) * tq)
        q2 = jnp.concatenate([q_ref[0, rows, :kw], q_ref[0, rows, kw:]], axis=0)
        keys = k_ref[0, c * ATT_KEY_CHUNK:(c + 1) * ATT_KEY_CHUNK, :]
        return lax.dot_general(q2, keys, (((1,), (1,)), ((), ())), preferred_element_type=F32)

    ss = [scores(item) for item in items[:ATT_LOOKAHEAD]]
    m = l = acc = None
    for idx, (t, c) in enumerate(items):
        if idx + ATT_LOOKAHEAD < len(items):
            ss.append(scores(items[idx + ATT_LOOKAHEAD]))
        s = ss[idx]
        mc = s[:, :LANES]
        for g in range(1, groups):
            mc = jnp.maximum(mc, s[:, g * LANES:(g + 1) * LANES])
        mc = jnp.max(mc, axis=-1, keepdims=True)
        m_new = mc if c == 0 else jnp.maximum(m, mc)
        p = jnp.exp2(s - m_new)
        ps = p[:, :LANES]
        for g in range(1, groups):
            ps = ps + p[:, g * LANES:(g + 1) * LANES]
        pv = _dot(p.astype(BF16), v_ref[0, c * ATT_KEY_CHUNK:(c + 1) * ATT_KEY_CHUNK, :])
        if c == 0:
            l, acc = ps, pv
        else:
            alpha = jnp.exp2(m - m_new)
            l = alpha * l + ps
            acc = alpha * acc + pv
        m = m_new
        if c == nchunk - 1:
            out = acc / jnp.sum(l, axis=-1, keepdims=True)
            o_ref[0, t * tq:(t + 1) * tq, :] = jnp.where(lane < HEAD_DIM, out[:tq], out[tq:]).astype(o_ref.dtype)


def _attention(q, k, v, k_shared):
    bsz, seq, _ = q.shape
    kw = k.shape[2] if k_shared else k.shape[2] // 2
    tq = ATT_Q_TILE * ATT_TILES_PER_STEP
    k_map = (lambda b, p, i: (b, 0, 0)) if k_shared else (lambda b, p, i: (b, 0, p))
    return pl.pallas_call(
        _attention_kernel,
        grid=(bsz, 2, seq // tq),
        in_specs=[pl.BlockSpec((1, tq, 2 * kw), lambda b, p, i: (b, i, p)),
                  pl.BlockSpec((1, seq, kw), k_map),
                  pl.BlockSpec((1, seq, LANES), lambda b, p, i: (b, 0, p))],
        out_specs=pl.BlockSpec((1, tq, LANES), lambda b, p, i: (b, i, p)),
        out_shape=jax.ShapeDtypeStruct((bsz, seq, 2 * LANES), BF16),
        compiler_params=pltpu.CompilerParams(dimension_semantics=("parallel", "parallel", "parallel"),
                                             vmem_limit_bytes=VMEM_LIMIT_BYTES),
        name="attention_shared_k" if k_shared else "attention",
    )(q, k, v)


def _mixout_kernel(a_ref, a_prev_ref, a_next_ref, cw_ref, cb_ref, ga_ref, ba_ref,
                   ob_ref, oc_ref, od_ref, w_ref, h_ref, g_ref, b_ref, o_ref, ext_ref, shift_ref,
                   *, tiles_per_seq):
    tm = a_ref.shape[0]
    i = pl.program_id(0) % tiles_per_seq
    has_prev = jnp.where(i > 0, 1.0, 0.0).astype(F32)
    has_next = jnp.where(i < tiles_per_seq - 1, 1.0, 0.0).astype(F32)
    ext_ref[0:CONV_HALO, :] = a_prev_ref[...] * has_prev
    ext_ref[CONV_HALO:CONV_HALO + tm, :] = a_ref[...]
    ext_ref[CONV_HALO + tm:, :] = a_next_ref[...] * has_next
    mix = _dot(ob_ref[...], w_ref[GROUP_W:2 * GROUP_W, :])
    mix = mix + _dot(oc_ref[...], w_ref[2 * GROUP_W:3 * GROUP_W, :])
    mix = mix + _dot(od_ref[...], w_ref[3 * GROUP_W:, :])

    half = CONV_A_WIDTH // 2
    span = tm + 2 * CONV_HALO - F32_SUBLANES
    acc = None
    for r in range(F32_SUBLANES):
        shift_ref[r] = ext_ref[r:r + span, :]
        for k in range(CONV_A_WIDTH):
            start = CONV_HALO - half + k
            if start % F32_SUBLANES == r:
                term = shift_ref[r, start - r:start - r + tm, :] * cw_ref[k:k + 1, :]
                acc = term if acc is None else acc + term
    conv = acc + cb_ref[...]
    o_a = jax.nn.silu(_layer_norm(conv, ga_ref[...], ba_ref[...])).astype(BF16)

    mix = mix + _dot(o_a, w_ref[0:GROUP_W, :])
    o_ref[...] = _layer_norm(DEEPNORM_ALPHA * h_ref[...] + mix, g_ref[...], b_ref[...])


def _mixout(a, o_b, o_c, o_d, h, seq, lw):
    t = h.shape[0]
    tm = TOKEN_TILE
    spt = seq // tm
    hb = tm // CONV_HALO
    last = t // CONV_HALO - 1
    tok = lambda width: pl.BlockSpec((tm, width), lambda i: (i, 0))
    full = lambda arr: pl.BlockSpec(arr.shape, lambda i: (0,) * arr.ndim)
    consts1 = [lw["conv_a_w"], lw["conv_a_b"], lw["ln_a_g"], lw["ln_a_b"]]
    consts2 = [lw["ln_mix_g"], lw["ln_mix_b"]]
    return pl.pallas_call(
        partial(_mixout_kernel, tiles_per_seq=spt),
        grid=(t // tm,),
        in_specs=[tok(GROUP_W),
                  pl.BlockSpec((CONV_HALO, GROUP_W), lambda i: (jnp.maximum(i * hb - 1, 0), 0)),
                  pl.BlockSpec((CONV_HALO, GROUP_W), lambda i: (jnp.minimum((i + 1) * hb, last), 0))]
                 + [full(c) for c in consts1]
                 + [tok(GROUP_W), tok(GROUP_W), tok(GROUP_W), full(lw["w_out"]), tok(D_MODEL)]
                 + [full(c) for c in consts2],
        out_specs=tok(D_MODEL),
        out_shape=jax.ShapeDtypeStruct((t, D_MODEL), F32),
        scratch_shapes=[pltpu.VMEM((tm + 2 * CONV_HALO, GROUP_W), F32),
                        pltpu.VMEM((F32_SUBLANES, tm + 2 * CONV_HALO - F32_SUBLANES, GROUP_W), F32)],
        compiler_params=pltpu.CompilerParams(dimension_semantics=("parallel",),
                                             vmem_limit_bytes=VMEM_LIMIT_BYTES),
        name="mixout",
    )(a, a, a, *consts1, o_b, o_c, o_d, lw["w_out"], h, *consts2)


def _ffn_kernel(h_ref, h_prev_ref, h_next_ref, w_up_ref, cw_ref, cb_ref, w_down_ref, g_ref, b_ref,
                o_ref, ext_ref, act_ref, *, tiles_per_seq):
    tm = h_ref.shape[0]
    i = pl.program_id(0) % tiles_per_seq
    has_prev = jnp.where(i > 0, 1.0, 0.0).astype(F32)
    has_next = jnp.where(i < tiles_per_seq - 1, 1.0, 0.0).astype(F32)
    ext_ref[0:tm, :] = h_ref[...].astype(BF16)
    ext_ref[tm:, :] = jnp.concatenate([h_prev_ref[...] * has_prev, h_next_ref[...] * has_next],
                                      axis=0).astype(BF16)
    hext = ext_ref[...]
    row8 = lax.broadcasted_iota(jnp.int32, (F32_SUBLANES, FF_CHUNK), 0)

    def up_proj(col):
        return _dot(hext, w_up_ref[:, col:col + FF_CHUNK])

    def conv(up, col):
        mid = up[:tm]
        before = tm + F32_SUBLANES - 1
        after = tm + F32_SUBLANES
        dn = pltpu.roll(mid, 1, 0)
        dn = jnp.concatenate([jnp.where(row8 == 0, up[before:before + 1], dn[:F32_SUBLANES]),
                              dn[F32_SUBLANES:]], axis=0)
        nx = pltpu.roll(mid, tm - 1, 0)
        nx = jnp.concatenate([nx[:tm - F32_SUBLANES],
                              jnp.where(row8 == F32_SUBLANES - 1, up[after:after + 1], nx[tm - F32_SUBLANES:])],
                             axis=0)
        w = cw_ref[:, col:col + FF_CHUNK]
        return dn * w[0:1] + mid * w[1:2] + nx * w[2:3] + cb_ref[:, col:col + FF_CHUNK]

    nchunk = D_FF // FF_CHUNK
    ups = [(up_proj(j * FF_CHUNK), up_proj(D_FF + j * FF_CHUNK)) for j in range(FFN_LOOKAHEAD)]
    down = None
    done = 0
    for j in range(nchunk):
        col = j * FF_CHUNK
        if j + FFN_LOOKAHEAD < nchunk:
            ahead = col + FFN_LOOKAHEAD * FF_CHUNK
            ups.append((up_proj(ahead), up_proj(D_FF + ahead)))
        cur = ups[j]
        act_ref[:, col:col + FF_CHUNK] = (jax.nn.silu(conv(cur[0], col)) * conv(cur[1], D_FF + col)).astype(BF16)
        if j + 1 in FFN_DOWN_AFTER:
            lo, hi = done * FF_CHUNK, (j + 1) * FF_CHUNK
            part = _dot(act_ref[:, lo:hi], w_down_ref[lo:hi, :])
            down = part if down is None else down + part
            done = j + 1
    o_ref[...] = _layer_norm(DEEPNORM_ALPHA * h_ref[...] + down, g_ref[...], b_ref[...])


def _ffn(h, seq, lw):
    t = h.shape[0]
    tm = TOKEN_TILE
    spt = seq // tm
    hb = tm // FFN_HALO
    last = t // FFN_HALO - 1
    tok = pl.BlockSpec((tm, D_MODEL), lambda i: (i, 0))
    full = lambda arr: pl.BlockSpec(arr.shape, lambda i: (0,) * arr.ndim)
    resident = lambda arr: pl.BlockSpec(arr.shape, lambda i: (0,) * arr.ndim, pipeline_mode=pl.Buffered(1))
    return pl.pallas_call(
        partial(_ffn_kernel, tiles_per_seq=spt),
        grid=(t // tm,),
        in_specs=[tok,
                  pl.BlockSpec((FFN_HALO, D_MODEL), lambda i: (jnp.maximum(i * hb - 1, 0), 0)),
                  pl.BlockSpec((FFN_HALO, D_MODEL), lambda i: (jnp.minimum((i + 1) * hb, last), 0)),
                  resident(lw["ffn_w_up"]), full(lw["ffn_conv_w"]), full(lw["ffn_conv_b"]),
                  resident(lw["ffn_w_down"]), full(lw["ln_ffn_g"]), full(lw["ln_ffn_b"])],
        out_specs=tok,
        out_shape=jax.ShapeDtypeStruct((t, D_MODEL), F32),
        scratch_shapes=[pltpu.VMEM((tm + 2 * FFN_HALO, D_MODEL), BF16),
                        pltpu.VMEM((tm, D_FF), BF16)],
        compiler_params=pltpu.CompilerParams(dimension_semantics=("parallel",),
                                             vmem_limit_bytes=VMEM_LIMIT_BYTES),
        name="ffn",
    )(h, h, h, lw["ffn_w_up"], lw["ffn_conv_w"], lw["ffn_conv_b"], lw["ffn_w_down"],
      lw["ln_ffn_g"], lw["ln_ffn_b"])


def _rope_tables(seq):
    rows = seq // GRID_W
    per_row = lambda x: jnp.repeat(x, GRID_W, axis=0)
    per_col = lambda x: jnp.tile(x, (rows, 1))

    def tables(half):
        inv_freq = ROPE_THETA ** (-jnp.arange(half, dtype=F32) / half)
        ar = jnp.arange(rows, dtype=F32)[:, None] * inv_freq[None, :]
        ac = jnp.arange(GRID_W, dtype=F32)[:, None] * inv_freq[None, :]
        cr, sr, cc, sc = per_row(jnp.cos(ar)), per_row(jnp.sin(ar)), per_col(jnp.cos(ac)), per_col(jnp.sin(ac))
        return jnp.concatenate([cr, cr, cc, cc], axis=-1), jnp.concatenate([-sr, sr, -sc, sc], axis=-1)

    cos64, sin64 = tables(HEAD_DIM // 4)
    cos32, sin32 = tables(MLA_ROPE // 4)
    pad = LANES - MLA_ROPE
    return {
        "cos_g": jnp.tile(cos64, (1, 2)), "sin_g": jnp.tile(sin64, (1, 2)),
        "cos_m": jnp.concatenate([cos32, jnp.ones((seq, pad), F32)], axis=-1),
        "sin_m": jnp.concatenate([sin32, jnp.zeros((seq, pad), F32)], axis=-1),
    }


def _layer_weights(l, w_in, conv_a_w, conv_a_b, ln_a_g, ln_a_b, qk_norm_q, qk_norm_k, sgu_ln_g, sgu_ln_b,
                   sgu_w, sgu_b, mla_q_norm, mla_w_uq, mla_kv_norm, mla_w_ukv, w_out, ln_mix_g, ln_mix_b,
                   ffn_w_up, ffn_conv_w, ffn_conv_b, ffn_w_down, ln_ffn_g, ln_ffn_b):
    w = w_in[l]
    z = lambda n: jnp.zeros((D_MODEL, n), F32)
    end_cq = COL_CQ + MLA_Q_LORA
    w_in_e = jnp.concatenate([w[:, :end_cq], z(2 * LANES - MLA_Q_LORA),
                              w[:, end_cq:], z(LANES - MLA_ROPE)], axis=1).astype(BF16)
    assert w.shape[1] == end_cq + MLA_KV_LORA + MLA_ROPE and w_in_e.shape == (D_MODEL, IN_COLS)

    dq = MLA_NOPE + MLA_ROPE
    zq = lambda n: jnp.zeros((MLA_Q_LORA, n), F32)
    cols = []
    for h in range(4):
        nope = mla_w_uq[l][:, h * dq:h * dq + MLA_NOPE]
        rope = mla_w_uq[l][:, h * dq + MLA_NOPE:(h + 1) * dq]
        cols += ([nope, zq(MLA_NOPE)] if h % 2 == 0 else [zq(MLA_NOPE), nope]) + [rope, zq(LANES - MLA_ROPE)]
    w_uq = jnp.concatenate(cols, axis=1)
    w_uq = jnp.concatenate([w_uq, jnp.zeros((2 * LANES - MLA_Q_LORA, w_uq.shape[1]), F32)], axis=0).astype(BF16)

    dkv = MLA_NOPE + HEAD_DIM
    w_ukv = jnp.concatenate([mla_w_ukv[l][:, h * dkv:h * dkv + MLA_NOPE] for h in range(4)]
                            + [mla_w_ukv[l][:, h * dkv + MLA_NOPE:(h + 1) * dkv] for h in range(4)],
                            axis=1).astype(BF16)

    row = lambda v: v.reshape(1, -1)
    return {
        "w_in_e": w_in_e,
        "gq": row(jnp.tile(qk_norm_q[l], 2)), "gk": row(jnp.tile(qk_norm_k[l], 2)),
        "sgu_g": row(sgu_ln_g[l]), "sgu_b": row(sgu_ln_b[l]),
        "sgu_w": sgu_w[l].reshape(SGU_GROUPS * CHUNK, CHUNK).astype(BF16),
        "sgu_bias": jnp.repeat(sgu_b[l].T, HEAD_DIM, axis=1),
        "gcq": row(jnp.concatenate([mla_q_norm[l], jnp.zeros((2 * LANES - MLA_Q_LORA,), F32)])),
        "w_uq": w_uq, "gckv": row(mla_kv_norm[l]), "w_ukv": w_ukv,
        "conv_a_w": conv_a_w[l], "conv_a_b": row(conv_a_b[l]), "ln_a_g": row(ln_a_g[l]), "ln_a_b": row(ln_a_b[l]),
        "w_out": w_out[l].astype(BF16), "ln_mix_g": row(ln_mix_g[l]), "ln_mix_b": row(ln_mix_b[l]),
        "ffn_w_up": ffn_w_up[l].astype(BF16), "ffn_conv_w": ffn_conv_w[l], "ffn_conv_b": row(ffn_conv_b[l]),
        "ffn_w_down": ffn_w_down[l].astype(BF16), "ln_ffn_g": row(ln_ffn_g[l]), "ln_ffn_b": row(ln_ffn_b[l]),
    }


def kernel(x, ln_in_g, ln_in_b, w_in, conv_a_w, conv_a_b, ln_a_g, ln_a_b, qk_norm_q, qk_norm_k, sgu_ln_g, sgu_ln_b, sgu_w, sgu_b, mla_q_norm, mla_w_uq, mla_kv_norm, mla_w_ukv, w_out, ln_mix_g, ln_mix_b, ffn_w_up, ffn_conv_w, ffn_conv_b, ffn_w_down, ln_ffn_g, ln_ffn_b):
    bsz, seq, d = x.shape
    assert d == D_MODEL and seq % TOKEN_TILE == 0 and seq % ATT_KEY_CHUNK == 0 and seq % GRID_W == 0
    t = bsz * seq
    tabs = _rope_tables(seq)
    h = x.reshape(t, d)
    for l in range(DEPTH):
        lw = _layer_weights(l, w_in, conv_a_w, conv_a_b, ln_a_g, ln_a_b, qk_norm_q, qk_norm_k, sgu_ln_g,
                            sgu_ln_b, sgu_w, sgu_b, mla_q_norm, mla_w_uq, mla_kv_norm, mla_w_ukv, w_out,
                            ln_mix_g, ln_mix_b, ffn_w_up, ffn_conv_w, ffn_conv_b, ffn_w_down, ln_ffn_g, ln_ffn_b)
        if l == 0:
            a, q_b, k_b, v_b, o_c, q_d, k_d, v_d, h = _inproj(
                h, seq, lw["w_in_e"], tabs, lw, input_ln=(ln_in_g.reshape(1, d), ln_in_b.reshape(1, d)))
        else:
            a, q_b, k_b, v_b, o_c, q_d, k_d, v_d = _inproj(h, seq, lw["w_in_e"], tabs, lw)
        sh = lambda arr: arr.reshape(bsz, seq, arr.shape[-1])
        o_b = _attention(sh(q_b), sh(k_b), sh(v_b), k_shared=True).reshape(t, GROUP_W)
        o_d = _attention(sh(q_d), sh(k_d), sh(v_d), k_shared=False).reshape(t, GROUP_W)
        h = _mixout(a, o_b, o_c, o_d, h, seq, lw)
        h = _ffn(h, seq, lw)
    return h.reshape(bsz, seq, d)
```

```python
from functools import partial

import jax
import jax.numpy as jnp
from jax import lax
from jax.experimental import pallas as pl
from jax.experimental.pallas import tpu as pltpu

F32 = jnp.float32
BF16 = jnp.bfloat16

D_MODEL = 1024
GRID_W = 64
GROUP_W = 256
HEAD_DIM = 64
CONV_A_WIDTH = 31
CHUNK = 128
SGU_GROUPS = 4
MLA_Q_LORA = 192
MLA_KV_LORA = 128
MLA_NOPE = 64
MLA_ROPE = 32
D_FF = 2816
DEPTH = 2
ROPE_THETA = 10000.0
DEEPNORM_ALPHA = (2 * DEPTH) ** 0.25
LN_EPS = 1e-5
RMS_EPS = 1e-6
LOG2_E = 1.4426950408889634

LANES = 128
F32_SUBLANES = 8
BF16_SUBLANES = 16
VMEM_LIMIT_BYTES = 52 * 1024 * 1024

TOKEN_TILE = 512
INPROJ_TILES_PER_STEP = 2
ATT_Q_TILE = 256
ATT_KEY_CHUNK = 512
ATT_TILES_PER_STEP = 4
ATT_LOOKAHEAD = 2
FF_CHUNK = 256
FFN_TILES_PER_STEP = 1
FFN_LOOKAHEAD = 2
FFN_DOWN_AFTER = (4, 8, 11)
CONV_HALO = 16
FFN_HALO = F32_SUBLANES

COL_A = 0
COL_Q = 512
COL_KV = 768
COL_C = 1024
COL_CQ = 1536
COL_CKV_KR = 1792
IN_COLS = 2048


def _layer_slab(stacked, layer, pipeline_mode=None):
    return pl.BlockSpec((None,) + stacked.shape[1:], lambda i: (layer, 0, 0), pipeline_mode=pipeline_mode)


def _dot(a, b):
    return jnp.dot(a, b, preferred_element_type=F32)


def _layer_norm(x, g, b):
    mu = jnp.mean(x, axis=-1, keepdims=True)
    xc = x - mu
    var = jnp.mean(xc * xc, axis=-1, keepdims=True)
    return xc * lax.rsqrt(var + LN_EPS) * g + b


def _rope(x, cos, sin_signed, half):
    lane = lax.broadcasted_iota(jnp.int32, x.shape, 1)
    first = (lane & (2 * half - 1)) < half
    partner = jnp.where(first, pltpu.roll(x, LANES - half, 1), pltpu.roll(x, half, 1))
    return x * cos + partner * sin_signed


INPROJ_PER_TOKEN_REFS = (1, 2, 3, 4)
INPROJ_NUM_OUTPUTS = 8


def _inproj_tiles(block_rows, refs):
    first_out = len(refs) - INPROJ_NUM_OUTPUTS
    for start in range(0, block_rows, TOKEN_TILE):
        rows = pl.ds(start, TOKEN_TILE)
        yield rows, [r.at[rows] if (i in INPROJ_PER_TOKEN_REFS or i >= first_out) else r
                     for i, r in enumerate(refs)]


def _inproj_kernel(h_ref, *refs):
    for rows, tile_refs in _inproj_tiles(h_ref.shape[0], refs):
        _inproj_body(h_ref[rows, :], *tile_refs)


def _ln_inproj_kernel(x_ref, ln_g_ref, ln_b_ref, *refs):
    h_out_ref = refs[-1]
    for rows, tile_refs in _inproj_tiles(x_ref.shape[0], refs[:-1]):
        h = _layer_norm(x_ref[rows, :], ln_g_ref[...], ln_b_ref[...])
        h_out_ref[rows, :] = h
        _inproj_body(h, *tile_refs)


def _inproj_body(h, w_ref, cos_g_ref, sin_g_ref, cos_m_ref, sin_m_ref,
                 gq_ref, gk_ref, sgu_g_ref, sgu_b_ref, sgu_w_ref, sgu_bias_ref,
                 gcq_ref, w_uq_ref, gckv_ref, w_ukv_ref,
                 a_ref, qb_ref, kb_ref, vb_ref, oc_ref, qd_ref, kd_ref, vd_ref):
    tm = h.shape[0]
    hb = h.astype(BF16)

    def proj(col, width):
        return _dot(hb, w_ref[:, col:col + width])

    pc = proj(COL_C, 2 * GROUP_W)
    pcq = proj(COL_CQ, 2 * LANES)
    pckv_kr = proj(COL_CKV_KR, 2 * LANES)
    pckv, pkr = pckv_kr[:, :LANES], pckv_kr[:, LANES:]
    pq = proj(COL_Q, 2 * LANES)
    pkv = proj(COL_KV, 2 * LANES)
    pk, pv = pkv[:, :LANES], pkv[:, LANES:]

    pc = jax.nn.gelu(pc, approximate=True)
    u = pc[:, :GROUP_W]
    sv = _layer_norm(pc[:, GROUP_W:], sgu_g_ref[...], sgu_b_ref[...]).astype(BF16)
    gates = [_dot(sgu_w_ref[...], sv[n * CHUNK:(n + 1) * CHUNK, :]) for n in range(tm // CHUNK)]

    ms = jnp.sum(pcq * pcq, axis=-1, keepdims=True) * (1.0 / MLA_Q_LORA)
    cqn = (pcq * lax.rsqrt(ms + RMS_EPS) * gcq_ref[...]).astype(BF16)
    qds = [_dot(cqn, w_uq_ref[:, hd * 2 * LANES:(hd + 1) * 2 * LANES]) for hd in range(4)]
    ms = jnp.mean(pckv * pckv, axis=-1, keepdims=True)
    ckvn = (pckv * lax.rsqrt(ms + RMS_EPS) * gckv_ref[...]).astype(BF16)
    kvd = _dot(ckvn, w_ukv_ref[...])

    pa = proj(COL_A, 2 * GROUP_W)

    a_ref[...] = pa[:, :GROUP_W] * jax.nn.sigmoid(pa[:, GROUP_W:])

    cos_g, sin_g = cos_g_ref[...], sin_g_ref[...]
    cos_m, sin_m = cos_m_ref[...], sin_m_ref[...]
    low = lax.broadcasted_iota(jnp.int32, (tm, LANES), 1) < HEAD_DIM

    def norm_rope_heads(x, gain):
        sq = x * x
        ms_lo = jnp.sum(jnp.where(low, sq, 0.0), axis=-1, keepdims=True) * (1.0 / HEAD_DIM)
        ms_hi = jnp.sum(jnp.where(low, 0.0, sq), axis=-1, keepdims=True) * (1.0 / HEAD_DIM)
        rs = jnp.where(low, lax.rsqrt(ms_lo + RMS_EPS), lax.rsqrt(ms_hi + RMS_EPS))
        return _rope(x * rs * gain, cos_g, sin_g, HEAD_DIM // 4)

    for kv in range(2):
        xr = norm_rope_heads(pq[:, kv * LANES:(kv + 1) * LANES], gq_ref[...]) * (HEAD_DIM ** -0.5 * LOG2_E)
        swapped = pltpu.roll(xr, HEAD_DIM, 1)
        first, second = (xr, swapped) if kv == 0 else (swapped, xr)
        keep = low if kv == 0 else jnp.logical_not(low)
        qb_ref[:, (2 * kv) * LANES:(2 * kv + 1) * LANES] = jnp.where(keep, first, 0.0).astype(BF16)
        qb_ref[:, (2 * kv + 1) * LANES:(2 * kv + 2) * LANES] = jnp.where(keep, second, 0.0).astype(BF16)

    kb_ref[...] = norm_rope_heads(pk, gk_ref[...]).astype(BF16)
    pv_swapped = pltpu.roll(pv, HEAD_DIM, 1)
    vb_ref[:, :LANES] = jnp.where(low, pv, pv_swapped).astype(BF16)
    vb_ref[:, LANES:] = jnp.where(low, pv_swapped, pv).astype(BF16)

    lane = lax.broadcasted_iota(jnp.int32, (CHUNK, GROUP_W), 1)
    for n, r in enumerate(gates):
        rows = slice(n * CHUNK, (n + 1) * CHUNK)
        gate = r[3 * CHUNK:4 * CHUNK, :]
        for g in (2, 1, 0):
            gate = jnp.where(lane < (g + 1) * HEAD_DIM, r[g * CHUNK:(g + 1) * CHUNK, :], gate)
        oc_ref[rows, :] = (u[rows, :] * (gate + sgu_bias_ref[...])).astype(BF16)

    scale_d = (MLA_NOPE + MLA_ROPE) ** -0.5 * LOG2_E
    for hd, qd in enumerate(qds):
        qd_ref[:, hd * 2 * LANES:hd * 2 * LANES + LANES] = (qd[:, :LANES] * scale_d).astype(BF16)
        qr = _rope(qd[:, LANES:], cos_m, sin_m, MLA_ROPE // 4) * scale_d
        qd_ref[:, hd * 2 * LANES + LANES:(hd + 1) * 2 * LANES] = qr.astype(BF16)
    kr = _rope(pkr, cos_m, sin_m, MLA_ROPE // 4).astype(BF16)
    for pr in range(2):
        kd_ref[:, pr * 2 * LANES:pr * 2 * LANES + LANES] = kvd[:, pr * LANES:(pr + 1) * LANES].astype(BF16)
        kd_ref[:, pr * 2 * LANES + LANES:(pr + 1) * 2 * LANES] = kr
    vd_ref[...] = kvd[:, 2 * LANES:].astype(BF16)


def _inproj(h, seq, w_in_e, tabs, lw, input_ln=None):
    t = h.shape[0]
    tm = TOKEN_TILE * INPROJ_TILES_PER_STEP
    spt = seq // tm
    tok = lambda width: pl.BlockSpec((tm, width), lambda i: (i, 0))
    pos = pl.BlockSpec((tm, LANES), lambda i: (i % spt, 0))
    full = lambda arr: pl.BlockSpec(arr.shape, lambda i: (0,) * arr.ndim)
    consts = [lw["gq"], lw["gk"], lw["sgu_g"], lw["sgu_b"], lw["sgu_w"], lw["sgu_bias"],
              lw["gcq"], lw["w_uq"], lw["gckv"], lw["w_ukv"]]
    out_widths = [(GROUP_W, F32), (4 * LANES, BF16), (LANES, BF16), (2 * LANES, BF16), (GROUP_W, BF16),
                  (8 * LANES, BF16), (4 * LANES, BF16), (GROUP_W, BF16)]
    ln_args = [] if input_ln is None else list(input_ln)
    if input_ln is not None:
        out_widths = out_widths + [(D_MODEL, F32)]
    return pl.pallas_call(
        _inproj_kernel if input_ln is None else _ln_inproj_kernel,
        grid=(t // tm,),
        in_specs=[tok(D_MODEL)] + [full(c) for c in ln_args] + [full(w_in_e), pos, pos, pos, pos]
                 + [full(c) for c in consts],
        out_specs=[tok(w) for w, _ in out_widths],
        out_shape=[jax.ShapeDtypeStruct((t, w), dt) for w, dt in out_widths],
        compiler_params=pltpu.CompilerParams(dimension_semantics=("parallel",),
                                             vmem_limit_bytes=VMEM_LIMIT_BYTES),
        name="inproj" if input_ln is None else "ln_inproj",
    )(h, *ln_args, w_in_e, tabs["cos_g"], tabs["sin_g"], tabs["cos_m"], tabs["sin_m"], *consts)


def _attention_kernel(q_ref, k_ref, v_ref, o_ref):
    tq = ATT_Q_TILE
    kw = k_ref.shape[2]
    seq = k_ref.shape[1]
    nchunk = seq // ATT_KEY_CHUNK
    groups = ATT_KEY_CHUNK // LANES
    lane = lax.broadcasted_iota(jnp.int32, (tq, LANES), 1)
    items = [(t, c) for t in range(q_ref.shape[1] // tq) for c in range(nchunk)]

    def scores(item):
        t, c = item
        rows = slice(t * tq, (t + 1) * tq)
        q2 = jnp.concatenate([q_ref[0, rows, :kw], q_ref[0, rows, kw:]], axis=0)
        keys = k_ref[0, c * ATT_KEY_CHUNK:(c + 1) * ATT_KEY_CHUNK, :]
        return lax.dot_general(q2, keys, (((1,), (1,)), ((), ())), preferred_element_type=F32)

    ss = [scores(item) for item in items[:ATT_LOOKAHEAD]]
    m = l = acc = None
    for idx, (t, c) in enumerate(items):
        if idx + ATT_LOOKAHEAD < len(items):
            ss.append(scores(items[idx + ATT_LOOKAHEAD]))
        s = ss[idx]
        mc = s[:, :LANES]
        for g in range(1, groups):
            mc = jnp.maximum(mc, s[:, g * LANES:(g + 1) * LANES])
        mc = jnp.max(mc, axis=-1, keepdims=True)
        m_new = mc if c == 0 else jnp.maximum(m, mc)
        p = jnp.exp2(s - m_new)
        ps = p[:, :LANES]
        for g in range(1, groups):
            ps = ps + p[:, g * LANES:(g + 1) * LANES]
        pv = _dot(p.astype(BF16), v_ref[0, c * ATT_KEY_CHUNK:(c + 1) * ATT_KEY_CHUNK, :])
        if c == 0:
            l, acc = ps, pv
        else:
            alpha = jnp.exp2(m - m_new)
            l = alpha * l + ps
            acc = alpha * acc + pv
        m = m_new
        if c == nchunk - 1:
            out = acc / jnp.sum(l, axis=-1, keepdims=True)
            o_ref[0, t * tq:(t + 1) * tq, :] = jnp.where(lane < HEAD_DIM, out[:tq], out[tq:]).astype(o_ref.dtype)


def _attention(q, k, v, k_shared):
    bsz, seq, _ = q.shape
    kw = k.shape[2] if k_shared else k.shape[2] // 2
    tq = ATT_Q_TILE * ATT_TILES_PER_STEP
    k_map = (lambda b, p, i: (b, 0, 0)) if k_shared else (lambda b, p, i: (b, 0, p))
    return pl.pallas_call(
        _attention_kernel,
        grid=(bsz, 2, seq // tq),
        in_specs=[pl.BlockSpec((1, tq, 2 * kw), lambda b, p, i: (b, i, p)),
                  pl.BlockSpec((1, seq, kw), k_map),
                  pl.BlockSpec((1, seq, LANES), lambda b, p, i: (b, 0, p))],
        out_specs=pl.BlockSpec((1, tq, LANES), lambda b, p, i: (b, i, p)),
        out_shape=jax.ShapeDtypeStruct((bsz, seq, 2 * LANES), BF16),
        compiler_params=pltpu.CompilerParams(dimension_semantics=("parallel", "parallel", "parallel"),
                                             vmem_limit_bytes=VMEM_LIMIT_BYTES),
        name="attention_shared_k" if k_shared else "attention",
    )(q, k, v)


def _mixout_kernel(a_ref, a_prev_ref, a_next_ref, cw_ref, cb_ref, ga_ref, ba_ref,
                   ob_ref, oc_ref, od_ref, w_ref, h_ref, g_ref, b_ref, o_ref, ext_ref, shift_ref,
                   *, tiles_per_seq):
    tm = a_ref.shape[0]
    i = pl.program_id(0) % tiles_per_seq
    has_prev = jnp.where(i > 0, 1.0, 0.0).astype(F32)
    has_next = jnp.where(i < tiles_per_seq - 1, 1.0, 0.0).astype(F32)
    ext_ref[0:CONV_HALO, :] = a_prev_ref[...] * has_prev
    ext_ref[CONV_HALO:CONV_HALO + tm, :] = a_ref[...]
    ext_ref[CONV_HALO + tm:, :] = a_next_ref[...] * has_next
    mix = _dot(ob_ref[...], w_ref[GROUP_W:2 * GROUP_W, :])
    mix = mix + _dot(oc_ref[...], w_ref[2 * GROUP_W:3 * GROUP_W, :])
    mix = mix + _dot(od_ref[...], w_ref[3 * GROUP_W:, :])

    half = CONV_A_WIDTH // 2
    span = tm + 2 * CONV_HALO - F32_SUBLANES
    acc = None
    for r in range(F32_SUBLANES):
        shift_ref[r] = ext_ref[r:r + span, :]
        for k in range(CONV_A_WIDTH):
            start = CONV_HALO - half + k
            if start % F32_SUBLANES == r:
                term = shift_ref[r, start - r:start - r + tm, :] * cw_ref[k:k + 1, :]
                acc = term if acc is None else acc + term
    conv = acc + cb_ref[...]
    o_a = jax.nn.silu(_layer_norm(conv, ga_ref[...], ba_ref[...])).astype(BF16)

    mix = mix + _dot(o_a, w_ref[0:GROUP_W, :])
    o_ref[...] = _layer_norm(DEEPNORM_ALPHA * h_ref[...] + mix, g_ref[...], b_ref[...])


def _mixout(a, o_b, o_c, o_d, h, seq, lw):
    t = h.shape[0]
    tm = TOKEN_TILE
    spt = seq // tm
    hb = tm // CONV_HALO
    last = t // CONV_HALO - 1
    tok = lambda width: pl.BlockSpec((tm, width), lambda i: (i, 0))
    full = lambda arr: pl.BlockSpec(arr.shape, lambda i: (0,) * arr.ndim)
    consts1 = [lw["conv_a_w"], lw["conv_a_b"], lw["ln_a_g"], lw["ln_a_b"]]
    consts2 = [lw["ln_mix_g"], lw["ln_mix_b"]]
    return pl.pallas_call(
        partial(_mixout_kernel, tiles_per_seq=spt),
        grid=(t // tm,),
        in_specs=[tok(GROUP_W),
                  pl.BlockSpec((CONV_HALO, GROUP_W), lambda i: (jnp.maximum(i * hb - 1, 0), 0)),
                  pl.BlockSpec((CONV_HALO, GROUP_W), lambda i: (jnp.minimum((i + 1) * hb, last), 0))]
                 + [full(c) for c in consts1]
                 + [tok(GROUP_W), tok(GROUP_W), tok(GROUP_W), _layer_slab(lw["w_out"], lw["layer"]), tok(D_MODEL)]
                 + [full(c) for c in consts2],
        out_specs=tok(D_MODEL),
        out_shape=jax.ShapeDtypeStruct((t, D_MODEL), F32),
        scratch_shapes=[pltpu.VMEM((tm + 2 * CONV_HALO, GROUP_W), F32),
                        pltpu.VMEM((F32_SUBLANES, tm + 2 * CONV_HALO - F32_SUBLANES, GROUP_W), F32)],
        compiler_params=pltpu.CompilerParams(dimension_semantics=("parallel",),
                                             vmem_limit_bytes=VMEM_LIMIT_BYTES),
        name="mixout",
    )(a, a, a, *consts1, o_b, o_c, o_d, lw["w_out"], h, *consts2)


def _ffn_kernel(h_ref, h_prev_ref, h_next_ref, w_up_ref, cw_ref, cb_ref, w_down_ref, g_ref, b_ref,
                o_ref, ext_ref, act_ref, *, blocks_per_seq):
    tm = TOKEN_TILE
    ntile = h_ref.shape[0] // tm
    i = pl.program_id(0) % blocks_per_seq
    has_prev = jnp.where(i > 0, 1.0, 0.0).astype(F32)
    has_next = jnp.where(i < blocks_per_seq - 1, 1.0, 0.0).astype(F32)
    for t in range(ntile):
        before = h_ref[t * tm - FFN_HALO:t * tm, :] if t > 0 else h_prev_ref[...] * has_prev
        after = h_ref[(t + 1) * tm:(t + 1) * tm + FFN_HALO, :] if t + 1 < ntile else h_next_ref[...] * has_next
        rows = pl.ds(t * tm, tm)
        _ffn_tile(h_ref[rows, :], before, after, w_up_ref, cw_ref, cb_ref, w_down_ref, g_ref, b_ref,
                  o_ref.at[rows], ext_ref.at[t], act_ref.at[t])


def _ffn_tile(h, before, after, w_up_ref, cw_ref, cb_ref, w_down_ref, g_ref, b_ref, o_ref, ext_ref, act_ref):
    tm = h.shape[0]
    ext_ref[0:tm, :] = h.astype(BF16)
    ext_ref[tm:, :] = jnp.concatenate([before, after], axis=0).astype(BF16)
    hext = ext_ref[...]
    row8 = lax.broadcasted_iota(jnp.int32, (F32_SUBLANES, FF_CHUNK), 0)

    def up_proj(col):
        return _dot(hext, w_up_ref[:, col:col + FF_CHUNK])

    def conv(up, col):
        mid = up[:tm]
        before = tm + F32_SUBLANES - 1
        after = tm + F32_SUBLANES
        dn = pltpu.roll(mid, 1, 0)
        dn = jnp.concatenate([jnp.where(row8 == 0, up[before:before + 1], dn[:F32_SUBLANES]),
                              dn[F32_SUBLANES:]], axis=0)
        nx = pltpu.roll(mid, tm - 1, 0)
        nx = jnp.concatenate([nx[:tm - F32_SUBLANES],
                              jnp.where(row8 == F32_SUBLANES - 1, up[after:after + 1], nx[tm - F32_SUBLANES:])],
                             axis=0)
        w = cw_ref[:, col:col + FF_CHUNK]
        return dn * w[0:1] + mid * w[1:2] + nx * w[2:3] + cb_ref[:, col:col + FF_CHUNK]

    nchunk = D_FF // FF_CHUNK
    ups = [(up_proj(j * FF_CHUNK), up_proj(D_FF + j * FF_CHUNK)) for j in range(FFN_LOOKAHEAD)]
    down = None
    done = 0
    for j in range(nchunk):
        col = j * FF_CHUNK
        if j + FFN_LOOKAHEAD < nchunk:
            ahead = col + FFN_LOOKAHEAD * FF_CHUNK
            ups.append((up_proj(ahead), up_proj(D_FF + ahead)))
        cur = ups[j]
        act_ref[:, col:col + FF_CHUNK] = (jax.nn.silu(conv(cur[0], col)) * conv(cur[1], D_FF + col)).astype(BF16)
        if j + 1 in FFN_DOWN_AFTER:
            lo, hi = done * FF_CHUNK, (j + 1) * FF_CHUNK
            part = _dot(act_ref[:, lo:hi], w_down_ref[lo:hi, :])
            down = part if down is None else down + part
            done = j + 1
    o_ref[...] = _layer_norm(DEEPNORM_ALPHA * h + down, g_ref[...], b_ref[...])


def _ffn(h, seq, lw):
    t = h.shape[0]
    tm = TOKEN_TILE * FFN_TILES_PER_STEP
    spt = seq // tm
    hb = tm // FFN_HALO
    last = t // FFN_HALO - 1
    tok = pl.BlockSpec((tm, D_MODEL), lambda i: (i, 0))
    full = lambda arr: pl.BlockSpec(arr.shape, lambda i: (0,) * arr.ndim)
    return pl.pallas_call(
        partial(_ffn_kernel, blocks_per_seq=spt),
        grid=(t // tm,),
        in_specs=[tok,
                  pl.BlockSpec((FFN_HALO, D_MODEL), lambda i: (jnp.maximum(i * hb - 1, 0), 0)),
                  pl.BlockSpec((FFN_HALO, D_MODEL), lambda i: (jnp.minimum((i + 1) * hb, last), 0)),
                  _layer_slab(lw["ffn_w_up"], lw["layer"], pl.Buffered(1)),
                  full(lw["ffn_conv_w"]), full(lw["ffn_conv_b"]),
                  _layer_slab(lw["ffn_w_down"], lw["layer"], pl.Buffered(1)),
                  full(lw["ln_ffn_g"]), full(lw["ln_ffn_b"])],
        out_specs=tok,
        out_shape=jax.ShapeDtypeStruct((t, D_MODEL), F32),
        scratch_shapes=[pltpu.VMEM((FFN_TILES_PER_STEP, TOKEN_TILE + 2 * FFN_HALO, D_MODEL), BF16),
                        pltpu.VMEM((FFN_TILES_PER_STEP, TOKEN_TILE, D_FF), BF16)],
        compiler_params=pltpu.CompilerParams(dimension_semantics=("parallel",),
                                             vmem_limit_bytes=VMEM_LIMIT_BYTES),
        name="ffn",
    )(h, h, h, lw["ffn_w_up"], lw["ffn_conv_w"], lw["ffn_conv_b"], lw["ffn_w_down"],
      lw["ln_ffn_g"], lw["ln_ffn_b"])


def _rope_tables(seq):
    rows = seq // GRID_W
    per_row = lambda x: jnp.repeat(x, GRID_W, axis=0)
    per_col = lambda x: jnp.tile(x, (rows, 1))

    def tables(half):
        inv_freq = ROPE_THETA ** (-jnp.arange(half, dtype=F32) / half)
        ar = jnp.arange(rows, dtype=F32)[:, None] * inv_freq[None, :]
        ac = jnp.arange(GRID_W, dtype=F32)[:, None] * inv_freq[None, :]
        cr, sr, cc, sc = per_row(jnp.cos(ar)), per_row(jnp.sin(ar)), per_col(jnp.cos(ac)), per_col(jnp.sin(ac))
        return jnp.concatenate([cr, cr, cc, cc], axis=-1), jnp.concatenate([-sr, sr, -sc, sc], axis=-1)

    cos64, sin64 = tables(HEAD_DIM // 4)
    cos32, sin32 = tables(MLA_ROPE // 4)
    pad = LANES - MLA_ROPE
    return {
        "cos_g": jnp.tile(cos64, (1, 2)), "sin_g": jnp.tile(sin64, (1, 2)),
        "cos_m": jnp.concatenate([cos32, jnp.ones((seq, pad), F32)], axis=-1),
        "sin_m": jnp.concatenate([sin32, jnp.zeros((seq, pad), F32)], axis=-1),
    }


def _layer_weights(l, w_in, conv_a_w, conv_a_b, ln_a_g, ln_a_b, qk_norm_q, qk_norm_k, sgu_ln_g, sgu_ln_b,
                   sgu_w, sgu_b, mla_q_norm, mla_w_uq, mla_kv_norm, mla_w_ukv, w_out, ln_mix_g, ln_mix_b,
                   ffn_w_up, ffn_conv_w, ffn_conv_b, ffn_w_down, ln_ffn_g, ln_ffn_b):
    w = w_in[l]
    z = lambda n: jnp.zeros((D_MODEL, n), F32)
    end_cq = COL_CQ + MLA_Q_LORA
    w_in_e = jnp.concatenate([w[:, :end_cq], z(2 * LANES - MLA_Q_LORA),
                              w[:, end_cq:], z(LANES - MLA_ROPE)], axis=1).astype(BF16)
    assert w.shape[1] == end_cq + MLA_KV_LORA + MLA_ROPE and w_in_e.shape == (D_MODEL, IN_COLS)

    dq = MLA_NOPE + MLA_ROPE
    zq = lambda n: jnp.zeros((MLA_Q_LORA, n), F32)
    cols = []
    for h in range(4):
        nope = mla_w_uq[l][:, h * dq:h * dq + MLA_NOPE]
        rope = mla_w_uq[l][:, h * dq + MLA_NOPE:(h + 1) * dq]
        cols += ([nope, zq(MLA_NOPE)] if h % 2 == 0 else [zq(MLA_NOPE), nope]) + [rope, zq(LANES - MLA_ROPE)]
    w_uq = jnp.concatenate(cols, axis=1)
    w_uq = jnp.concatenate([w_uq, jnp.zeros((2 * LANES - MLA_Q_LORA, w_uq.shape[1]), F32)], axis=0).astype(BF16)

    dkv = MLA_NOPE + HEAD_DIM
    w_ukv = jnp.concatenate([mla_w_ukv[l][:, h * dkv:h * dkv + MLA_NOPE] for h in range(4)]
                            + [mla_w_ukv[l][:, h * dkv + MLA_NOPE:(h + 1) * dkv] for h in range(4)],
                            axis=1).astype(BF16)

    row = lambda v: v.reshape(1, -1)
    return {
        "w_in_e": w_in_e,
        "gq": row(jnp.tile(qk_norm_q[l], 2)), "gk": row(jnp.tile(qk_norm_k[l], 2)),
        "sgu_g": row(sgu_ln_g[l]), "sgu_b": row(sgu_ln_b[l]),
        "sgu_w": sgu_w[l].reshape(SGU_GROUPS * CHUNK, CHUNK).astype(BF16),
        "sgu_bias": jnp.repeat(sgu_b[l].T, HEAD_DIM, axis=1),
        "gcq": row(jnp.concatenate([mla_q_norm[l], jnp.zeros((2 * LANES - MLA_Q_LORA,), F32)])),
        "w_uq": w_uq, "gckv": row(mla_kv_norm[l]), "w_ukv": w_ukv,
        "conv_a_w": conv_a_w[l], "conv_a_b": row(conv_a_b[l]), "ln_a_g": row(ln_a_g[l]), "ln_a_b": row(ln_a_b[l]),
        "layer": l, "w_out": w_out.astype(BF16), "ln_mix_g": row(ln_mix_g[l]), "ln_mix_b": row(ln_mix_b[l]),
        "ffn_w_up": ffn_w_up.astype(BF16), "ffn_conv_w": ffn_conv_w[l], "ffn_conv_b": row(ffn_conv_b[l]),
        "ffn_w_down": ffn_w_down.astype(BF16), "ln_ffn_g": row(ln_ffn_g[l]), "ln_ffn_b": row(ln_ffn_b[l]),
    }


def kernel(x, ln_in_g, ln_in_b, w_in, conv_a_w, conv_a_b, ln_a_g, ln_a_b, qk_norm_q, qk_norm_k, sgu_ln_g, sgu_ln_b, sgu_w, sgu_b, mla_q_norm, mla_w_uq, mla_kv_norm, mla_w_ukv, w_out, ln_mix_g, ln_mix_b, ffn_w_up, ffn_conv_w, ffn_conv_b, ffn_w_down, ln_ffn_g, ln_ffn_b):
    bsz, seq, d = x.shape
    assert d == D_MODEL and seq % TOKEN_TILE == 0 and seq % ATT_KEY_CHUNK == 0 and seq % GRID_W == 0
    t = bsz * seq
    tabs = _rope_tables(seq)
    h = x.reshape(t, d)
    for l in range(DEPTH):
        lw = _layer_weights(l, w_in, conv_a_w, conv_a_b, ln_a_g, ln_a_b, qk_norm_q, qk_norm_k, sgu_ln_g,
                            sgu_ln_b, sgu_w, sgu_b, mla_q_norm, mla_w_uq, mla_kv_norm, mla_w_ukv, w_out,
                            ln_mix_g, ln_mix_b, ffn_w_up, ffn_conv_w, ffn_conv_b, ffn_w_down, ln_ffn_g, ln_ffn_b)
        if l == 0:
            a, q_b, k_b, v_b, o_c, q_d, k_d, v_d, h = _inproj(
                h, seq, lw["w_in_e"], tabs, lw, input_ln=(ln_in_g.reshape(1, d), ln_in_b.reshape(1, d)))
        else:
            a, q_b, k_b, v_b, o_c, q_d, k_d, v_d = _inproj(h, seq, lw["w_in_e"], tabs, lw)
        sh = lambda arr: arr.reshape(bsz, seq, arr.shape[-1])
        o_b = _attention(sh(q_b), sh(k_b), sh(v_b), k_shared=True).reshape(t, GROUP_W)
        o_d = _attention(sh(q_d), sh(k_d), sh(v_d), k_shared=False).reshape(t, GROUP_W)
        h = _mixout(a, o_b, o_c, o_d, h, seq, lw)
        h = _ffn(h, seq, lw)
    return h.reshape(bsz, seq, d)
```

```python
from functools import partial

import jax
import jax.numpy as jnp
from jax import lax
from jax.experimental import pallas as pl
from jax.experimental.pallas import tpu as pltpu

F32 = jnp.float32
BF16 = jnp.bfloat16

D_MODEL = 1024
GRID_W = 64
GROUP_W = 256
HEAD_DIM = 64
CONV_A_WIDTH = 31
CHUNK = 128
SGU_GROUPS = 4
MLA_Q_LORA = 192
MLA_KV_LORA = 128
MLA_NOPE = 64
MLA_ROPE = 32
D_FF = 2816
DEPTH = 2
ROPE_THETA = 10000.0
DEEPNORM_ALPHA = (2 * DEPTH) ** 0.25
LN_EPS = 1e-5
RMS_EPS = 1e-6
LOG2_E = 1.4426950408889634

LANES = 128
F32_SUBLANES = 8
BF16_SUBLANES = 16
VMEM_LIMIT_BYTES = 52 * 1024 * 1024

TOKEN_TILE = 512
INPROJ_TILES_PER_STEP = 2
ATT_Q_TILE = 256
ATT_KEY_CHUNK = 512
ATT_TILES_PER_STEP = 4
ATT_LOOKAHEAD = 2
FF_CHUNK = 256
FFN_TILES_PER_STEP = 1
FFN_LOOKAHEAD = 2
FFN_DOWN_AFTER = (4, 8, 11)
CONV_HALO = 16
FFN_HALO = F32_SUBLANES

COL_A = 0
COL_Q = 512
COL_KV = 768
COL_C = 1024
COL_CQ = 1536
COL_CKV_KR = 1792
IN_COLS = 2048


def _layer_slab(stacked, layer, pipeline_mode=None):
    return pl.BlockSpec((None,) + stacked.shape[1:], lambda i: (layer, 0, 0), pipeline_mode=pipeline_mode)


def _dot(a, b):
    return jnp.dot(a, b, preferred_element_type=F32)


def _layer_norm(x, g, b):
    mu = jnp.mean(x, axis=-1, keepdims=True)
    xc = x - mu
    var = jnp.mean(xc * xc, axis=-1, keepdims=True)
    return xc * lax.rsqrt(var + LN_EPS) * g + b


def _rope(x, cos, sin_signed, half):
    lane = lax.broadcasted_iota(jnp.int32, x.shape, 1)
    first = (lane & (2 * half - 1)) < half
    partner = jnp.where(first, pltpu.roll(x, LANES - half, 1), pltpu.roll(x, half, 1))
    return x * cos + partner * sin_signed


INPROJ_PER_TOKEN_REFS = (1, 2, 3, 4)
INPROJ_NUM_OUTPUTS = 8


def _inproj_tiles(block_rows, refs):
    first_out = len(refs) - INPROJ_NUM_OUTPUTS
    for start in range(0, block_rows, TOKEN_TILE):
        rows = pl.ds(start, TOKEN_TILE)
        yield rows, [r.at[rows] if (i in INPROJ_PER_TOKEN_REFS or i >= first_out) else r
                     for i, r in enumerate(refs)]


def _inproj_kernel(h_ref, *refs):
    for rows, tile_refs in _inproj_tiles(h_ref.shape[0], refs):
        _inproj_body(h_ref[rows, :], *tile_refs)


def _ln_inproj_kernel(x_ref, ln_g_ref, ln_b_ref, *refs):
    h_out_ref = refs[-1]
    for rows, tile_refs in _inproj_tiles(x_ref.shape[0], refs[:-1]):
        h = _layer_norm(x_ref[rows, :], ln_g_ref[...], ln_b_ref[...])
        h_out_ref[rows, :] = h
        _inproj_body(h, *tile_refs)


def _inproj_body(h, w_ref, cos_g_ref, sin_g_ref, cos_m_ref, sin_m_ref,
                 gq_ref, gk_ref, sgu_g_ref, sgu_b_ref, sgu_w_ref, sgu_bias_ref,
                 gcq_ref, w_uq_ref, gckv_ref, w_ukv_ref,
                 a_ref, qb_ref, kb_ref, vb_ref, oc_ref, qd_ref, kd_ref, vd_ref):
    tm = h.shape[0]
    hb = h.astype(BF16)

    def proj(col, width):
        return _dot(hb, w_ref[:, col:col + width])

    pc = proj(COL_C, 2 * GROUP_W)
    pcq = proj(COL_CQ, 2 * LANES)
    pckv_kr = proj(COL_CKV_KR, 2 * LANES)
    pckv, pkr = pckv_kr[:, :LANES], pckv_kr[:, LANES:]
    pq = proj(COL_Q, 2 * LANES)
    pkv = proj(COL_KV, 2 * LANES)
    pk, pv = pkv[:, :LANES], pkv[:, LANES:]

    pc = jax.nn.gelu(pc, approximate=True)
    u = pc[:, :GROUP_W]
    sv = _layer_norm(pc[:, GROUP_W:], sgu_g_ref[...], sgu_b_ref[...]).astype(BF16)
    gates = [_dot(sgu_w_ref[...], sv[n * CHUNK:(n + 1) * CHUNK, :]) for n in range(tm // CHUNK)]

    ms = jnp.sum(pcq * pcq, axis=-1, keepdims=True) * (1.0 / MLA_Q_LORA)
    cqn = (pcq * lax.rsqrt(ms + RMS_EPS) * gcq_ref[...]).astype(BF16)
    qds = [_dot(cqn, w_uq_ref[:, hd * 2 * LANES:(hd + 1) * 2 * LANES]) for hd in range(4)]
    ms = jnp.mean(pckv * pckv, axis=-1, keepdims=True)
    ckvn = (pckv * lax.rsqrt(ms + RMS_EPS) * gckv_ref[...]).astype(BF16)
    kvd = _dot(ckvn, w_ukv_ref[...])

    pa = proj(COL_A, 2 * GROUP_W)

    a_ref[...] = pa[:, :GROUP_W] * jax.nn.sigmoid(pa[:, GROUP_W:])

    cos_g, sin_g = cos_g_ref[...], sin_g_ref[...]
    cos_m, sin_m = cos_m_ref[...], sin_m_ref[...]
    low = lax.broadcasted_iota(jnp.int32, (tm, LANES), 1) < HEAD_DIM

    def norm_rope_heads(x, gain):
        sq = x * x
        ms_lo = jnp.sum(jnp.where(low, sq, 0.0), axis=-1, keepdims=True) * (1.0 / HEAD_DIM)
        ms_hi = jnp.sum(jnp.where(low, 0.0, sq), axis=-1, keepdims=True) * (1.0 / HEAD_DIM)
        rs = jnp.where(low, lax.rsqrt(ms_lo + RMS_EPS), lax.rsqrt(ms_hi + RMS_EPS))
        return _rope(x * rs * gain, cos_g, sin_g, HEAD_DIM // 4)

    for kv in range(2):
        xr = norm_rope_heads(pq[:, kv * LANES:(kv + 1) * LANES], gq_ref[...]) * (HEAD_DIM ** -0.5 * LOG2_E)
        swapped = pltpu.roll(xr, HEAD_DIM, 1)
        first, second = (xr, swapped) if kv == 0 else (swapped, xr)
        keep = low if kv == 0 else jnp.logical_not(low)
        qb_ref[:, (2 * kv) * LANES:(2 * kv + 1) * LANES] = jnp.where(keep, first, 0.0).astype(BF16)
        qb_ref[:, (2 * kv + 1) * LANES:(2 * kv + 2) * LANES] = jnp.where(keep, second, 0.0).astype(BF16)

    kb_ref[...] = norm_rope_heads(pk, gk_ref[...]).astype(BF16)
    ones = jnp.ones((tm, LANES), BF16)
    pv_swapped = pltpu.roll(pv, HEAD_DIM, 1)
    vb_ref[:, :LANES] = jnp.where(low, pv, pv_swapped).astype(BF16)
    vb_ref[:, LANES:2 * LANES] = ones
    vb_ref[:, 2 * LANES:3 * LANES] = jnp.where(low, pv_swapped, pv).astype(BF16)
    vb_ref[:, 3 * LANES:] = ones

    lane = lax.broadcasted_iota(jnp.int32, (CHUNK, GROUP_W), 1)
    for n, r in enumerate(gates):
        rows = slice(n * CHUNK, (n + 1) * CHUNK)
        gate = r[3 * CHUNK:4 * CHUNK, :]
        for g in (2, 1, 0):
            gate = jnp.where(lane < (g + 1) * HEAD_DIM, r[g * CHUNK:(g + 1) * CHUNK, :], gate)
        oc_ref[rows, :] = (u[rows, :] * (gate + sgu_bias_ref[...])).astype(BF16)

    scale_d = (MLA_NOPE + MLA_ROPE) ** -0.5 * LOG2_E
    for hd, qd in enumerate(qds):
        qd_ref[:, hd * 2 * LANES:hd * 2 * LANES + LANES] = (qd[:, :LANES] * scale_d).astype(BF16)
        qr = _rope(qd[:, LANES:], cos_m, sin_m, MLA_ROPE // 4) * scale_d
        qd_ref[:, hd * 2 * LANES + LANES:(hd + 1) * 2 * LANES] = qr.astype(BF16)
    kr = _rope(pkr, cos_m, sin_m, MLA_ROPE // 4).astype(BF16)
    for pr in range(2):
        kd_ref[:, pr * 2 * LANES:pr * 2 * LANES + LANES] = kvd[:, pr * LANES:(pr + 1) * LANES].astype(BF16)
        kd_ref[:, pr * 2 * LANES + LANES:(pr + 1) * 2 * LANES] = kr
    for pr in range(2):
        vd_ref[:, pr * 2 * LANES:pr * 2 * LANES + LANES] = kvd[:, (2 + pr) * LANES:(3 + pr) * LANES].astype(BF16)
        vd_ref[:, pr * 2 * LANES + LANES:(pr + 1) * 2 * LANES] = ones


def _inproj(h, seq, w_in_e, tabs, lw, input_ln=None):
    t = h.shape[0]
    tm = TOKEN_TILE * INPROJ_TILES_PER_STEP
    spt = seq // tm
    tok = lambda width: pl.BlockSpec((tm, width), lambda i: (i, 0))
    pos = pl.BlockSpec((tm, LANES), lambda i: (i % spt, 0))
    full = lambda arr: pl.BlockSpec(arr.shape, lambda i: (0,) * arr.ndim)
    consts = [lw["gq"], lw["gk"], lw["sgu_g"], lw["sgu_b"], lw["sgu_w"], lw["sgu_bias"],
              lw["gcq"], lw["w_uq"], lw["gckv"], lw["w_ukv"]]
    out_widths = [(GROUP_W, F32), (4 * LANES, BF16), (LANES, BF16), (4 * LANES, BF16), (GROUP_W, BF16),
                  (8 * LANES, BF16), (4 * LANES, BF16), (4 * LANES, BF16)]
    ln_args = [] if input_ln is None else list(input_ln)
    if input_ln is not None:
        out_widths = out_widths + [(D_MODEL, F32)]
    return pl.pallas_call(
        _inproj_kernel if input_ln is None else _ln_inproj_kernel,
        grid=(t // tm,),
        in_specs=[tok(D_MODEL)] + [full(c) for c in ln_args] + [full(w_in_e), pos, pos, pos, pos]
                 + [full(c) for c in consts],
        out_specs=[tok(w) for w, _ in out_widths],
        out_shape=[jax.ShapeDtypeStruct((t, w), dt) for w, dt in out_widths],
        compiler_params=pltpu.CompilerParams(dimension_semantics=("parallel",),
                                             vmem_limit_bytes=VMEM_LIMIT_BYTES),
        name="inproj" if input_ln is None else "ln_inproj",
    )(h, *ln_args, w_in_e, tabs["cos_g"], tabs["sin_g"], tabs["cos_m"], tabs["sin_m"], *consts)


def _attention_kernel(q_ref, k_ref, v_ref, o_ref):
    tq = ATT_Q_TILE
    kw = k_ref.shape[2]
    seq = k_ref.shape[1]
    nchunk = seq // ATT_KEY_CHUNK
    groups = ATT_KEY_CHUNK // LANES
    lane = lax.broadcasted_iota(jnp.int32, (tq, LANES), 1)
    items = [(t, c) for t in range(q_ref.shape[1] // tq) for c in range(nchunk)]

    def scores(item):
        t, c = item
        rows = slice(t * tq, (t + 1) * tq)
        q2 = jnp.concatenate([q_ref[0, rows, :kw], q_ref[0, rows, kw:]], axis=0)
        keys = k_ref[0, c * ATT_KEY_CHUNK:(c + 1) * ATT_KEY_CHUNK, :]
        return lax.dot_general(q2, keys, (((1,), (1,)), ((), ())), preferred_element_type=F32)

    ss = [scores(item) for item in items[:ATT_LOOKAHEAD]]
    m = acc = None
    for idx, (t, c) in enumerate(items):
        if idx + ATT_LOOKAHEAD < len(items):
            ss.append(scores(items[idx + ATT_LOOKAHEAD]))
        s = ss[idx]
        mc = s[:, :LANES]
        for g in range(1, groups):
            mc = jnp.maximum(mc, s[:, g * LANES:(g + 1) * LANES])
        mc = jnp.max(mc, axis=-1, keepdims=True)
        m_new = mc if c == 0 else jnp.maximum(m, mc)
        p = jnp.exp2(s - m_new).astype(BF16)
        pv = _dot(p, v_ref[0, c * ATT_KEY_CHUNK:(c + 1) * ATT_KEY_CHUNK, :])
        acc = pv if c == 0 else jnp.exp2(m - m_new) * acc + pv
        m = m_new
        if c == nchunk - 1:
            out = acc[:, :LANES] / acc[:, LANES:]
            o_ref[0, t * tq:(t + 1) * tq, :] = jnp.where(lane < HEAD_DIM, out[:tq], out[tq:]).astype(o_ref.dtype)


def _attention(q, k, v, k_shared):
    bsz, seq, _ = q.shape
    kw = k.shape[2] if k_shared else k.shape[2] // 2
    tq = ATT_Q_TILE * ATT_TILES_PER_STEP
    k_map = (lambda b, p, i: (b, 0, 0)) if k_shared else (lambda b, p, i: (b, 0, p))
    return pl.pallas_call(
        _attention_kernel,
        grid=(bsz, 2, seq // tq),
        in_specs=[pl.BlockSpec((1, tq, 2 * kw), lambda b, p, i: (b, i, p)),
                  pl.BlockSpec((1, seq, kw), k_map),
                  pl.BlockSpec((1, seq, 2 * LANES), lambda b, p, i: (b, 0, p))],
        out_specs=pl.BlockSpec((1, tq, LANES), lambda b, p, i: (b, i, p)),
        out_shape=jax.ShapeDtypeStruct((bsz, seq, 2 * LANES), BF16),
        compiler_params=pltpu.CompilerParams(dimension_semantics=("parallel", "parallel", "parallel"),
                                             vmem_limit_bytes=VMEM_LIMIT_BYTES),
        name="attention_shared_k" if k_shared else "attention",
    )(q, k, v)


def _mixout_kernel(a_ref, a_prev_ref, a_next_ref, cw_ref, cb_ref, ga_ref, ba_ref,
                   ob_ref, oc_ref, od_ref, w_ref, h_ref, g_ref, b_ref, o_ref, ext_ref, shift_ref,
                   *, tiles_per_seq):
    tm = a_ref.shape[0]
    i = pl.program_id(0) % tiles_per_seq
    has_prev = jnp.where(i > 0, 1.0, 0.0).astype(F32)
    has_next = jnp.where(i < tiles_per_seq - 1, 1.0, 0.0).astype(F32)
    ext_ref[0:CONV_HALO, :] = a_prev_ref[...] * has_prev
    ext_ref[CONV_HALO:CONV_HALO + tm, :] = a_ref[...]
    ext_ref[CONV_HALO + tm:, :] = a_next_ref[...] * has_next
    mix = _dot(ob_ref[...], w_ref[GROUP_W:2 * GROUP_W, :])
    mix = mix + _dot(oc_ref[...], w_ref[2 * GROUP_W:3 * GROUP_W, :])
    mix = mix + _dot(od_ref[...], w_ref[3 * GROUP_W:, :])

    half = CONV_A_WIDTH // 2
    span = tm + 2 * CONV_HALO - F32_SUBLANES
    acc = None
    for r in range(F32_SUBLANES):
        shift_ref[r] = ext_ref[r:r + span, :]
        for k in range(CONV_A_WIDTH):
            start = CONV_HALO - half + k
            if start % F32_SUBLANES == r:
                term = shift_ref[r, start - r:start - r + tm, :] * cw_ref[k:k + 1, :]
                acc = term if acc is None else acc + term
    conv = acc + cb_ref[...]
    o_a = jax.nn.silu(_layer_norm(conv, ga_ref[...], ba_ref[...])).astype(BF16)

    mix = mix + _dot(o_a, w_ref[0:GROUP_W, :])
    o_ref[...] = _layer_norm(DEEPNORM_ALPHA * h_ref[...] + mix, g_ref[...], b_ref[...])


def _mixout(a, o_b, o_c, o_d, h, seq, lw):
    t = h.shape[0]
    tm = TOKEN_TILE
    spt = seq // tm
    hb = tm // CONV_HALO
    last = t // CONV_HALO - 1
    tok = lambda width: pl.BlockSpec((tm, width), lambda i: (i, 0))
    full = lambda arr: pl.BlockSpec(arr.shape, lambda i: (0,) * arr.ndim)
    consts1 = [lw["conv_a_w"], lw["conv_a_b"], lw["ln_a_g"], lw["ln_a_b"]]
    consts2 = [lw["ln_mix_g"], lw["ln_mix_b"]]
    return pl.pallas_call(
        partial(_mixout_kernel, tiles_per_seq=spt),
        grid=(t // tm,),
        in_specs=[tok(GROUP_W),
                  pl.BlockSpec((CONV_HALO, GROUP_W), lambda i: (jnp.maximum(i * hb - 1, 0), 0)),
                  pl.BlockSpec((CONV_HALO, GROUP_W), lambda i: (jnp.minimum((i + 1) * hb, last), 0))]
                 + [full(c) for c in consts1]
                 + [tok(GROUP_W), tok(GROUP_W), tok(GROUP_W), _layer_slab(lw["w_out"], lw["layer"]), tok(D_MODEL)]
                 + [full(c) for c in consts2],
        out_specs=tok(D_MODEL),
        out_shape=jax.ShapeDtypeStruct((t, D_MODEL), F32),
        scratch_shapes=[pltpu.VMEM((tm + 2 * CONV_HALO, GROUP_W), F32),
                        pltpu.VMEM((F32_SUBLANES, tm + 2 * CONV_HALO - F32_SUBLANES, GROUP_W), F32)],
        compiler_params=pltpu.CompilerParams(dimension_semantics=("parallel",),
                                             vmem_limit_bytes=VMEM_LIMIT_BYTES),
        name="mixout",
    )(a, a, a, *consts1, o_b, o_c, o_d, lw["w_out"], h, *consts2)


def _ffn_kernel(h_ref, h_prev_ref, h_next_ref, w_up_ref, cw_ref, cb_ref, w_down_ref, g_ref, b_ref,
                o_ref, ext_ref, act_ref, *, blocks_per_seq):
    tm = TOKEN_TILE
    ntile = h_ref.shape[0] // tm
    i = pl.program_id(0) % blocks_per_seq
    has_prev = jnp.where(i > 0, 1.0, 0.0).astype(F32)
    has_next = jnp.where(i < blocks_per_seq - 1, 1.0, 0.0).astype(F32)
    for t in range(ntile):
        before = h_ref[t * tm - FFN_HALO:t * tm, :] if t > 0 else h_prev_ref[...] * has_prev
        after = h_ref[(t + 1) * tm:(t + 1) * tm + FFN_HALO, :] if t + 1 < ntile else h_next_ref[...] * has_next
        rows = pl.ds(t * tm, tm)
        _ffn_tile(h_ref[rows, :], before, after, w_up_ref, cw_ref, cb_ref, w_down_ref, g_ref, b_ref,
                  o_ref.at[rows], ext_ref.at[t], act_ref.at[t])


def _ffn_tile(h, before, after, w_up_ref, cw_ref, cb_ref, w_down_ref, g_ref, b_ref, o_ref, ext_ref, act_ref):
    tm = h.shape[0]
    ext_ref[0:tm, :] = h.astype(BF16)
    ext_ref[tm:, :] = jnp.concatenate([before, after], axis=0).astype(BF16)
    hext = ext_ref[...]
    row8 = lax.broadcasted_iota(jnp.int32, (F32_SUBLANES, FF_CHUNK), 0)

    def up_proj(col):
        return _dot(hext, w_up_ref[:, col:col + FF_CHUNK])

    def conv(up, col):
        mid = up[:tm]
        before = tm + F32_SUBLANES - 1
        after = tm + F32_SUBLANES
        dn = pltpu.roll(mid, 1, 0)
        dn = jnp.concatenate([jnp.where(row8 == 0, up[before:before + 1], dn[:F32_SUBLANES]),
                              dn[F32_SUBLANES:]], axis=0)
        nx = pltpu.roll(mid, tm - 1, 0)
        nx = jnp.concatenate([nx[:tm - F32_SUBLANES],
                              jnp.where(row8 == F32_SUBLANES - 1, up[after:after + 1], nx[tm - F32_SUBLANES:])],
                             axis=0)
        w = cw_ref[:, col:col + FF_CHUNK]
        return dn * w[0:1] + mid * w[1:2] + nx * w[2:3] + cb_ref[:, col:col + FF_CHUNK]

    nchunk = D_FF // FF_CHUNK
    ups = [(up_proj(j * FF_CHUNK), up_proj(D_FF + j * FF_CHUNK)) for j in range(FFN_LOOKAHEAD)]
    down = None
    done = 0
    for j in range(nchunk):
        col = j * FF_CHUNK
        if j + FFN_LOOKAHEAD < nchunk:
            ahead = col + FFN_LOOKAHEAD * FF_CHUNK
            ups.append((up_proj(ahead), up_proj(D_FF + ahead)))
        cur = ups[j]
        act_ref[:, col:col + FF_CHUNK] = (jax.nn.silu(conv(cur[0], col)) * conv(cur[1], D_FF + col)).astype(BF16)
        if j + 1 in FFN_DOWN_AFTER:
            lo, hi = done * FF_CHUNK, (j + 1) * FF_CHUNK
            part = _dot(act_ref[:, lo:hi], w_down_ref[lo:hi, :])
            down = part if down is None else down + part
            done = j + 1
    o_ref[...] = _layer_norm(DEEPNORM_ALPHA * h + down, g_ref[...], b_ref[...])


def _ffn(h, seq, lw):
    t = h.shape[0]
    tm = TOKEN_TILE * FFN_TILES_PER_STEP
    spt = seq // tm
    hb = tm // FFN_HALO
    last = t // FFN_HALO - 1
    tok = pl.BlockSpec((tm, D_MODEL), lambda i: (i, 0))
    full = lambda arr: pl.BlockSpec(arr.shape, lambda i: (0,) * arr.ndim)
    return pl.pallas_call(
        partial(_ffn_kernel, blocks_per_seq=spt),
        grid=(t // tm,),
        in_specs=[tok,
                  pl.BlockSpec((FFN_HALO, D_MODEL), lambda i: (jnp.maximum(i * hb - 1, 0), 0)),
                  pl.BlockSpec((FFN_HALO, D_MODEL), lambda i: (jnp.minimum((i + 1) * hb, last), 0)),
                  _layer_slab(lw["ffn_w_up"], lw["layer"], pl.Buffered(1)),
                  full(lw["ffn_conv_w"]), full(lw["ffn_conv_b"]),
                  _layer_slab(lw["ffn_w_down"], lw["layer"], pl.Buffered(1)),
                  full(lw["ln_ffn_g"]), full(lw["ln_ffn_b"])],
        out_specs=tok,
        out_shape=jax.ShapeDtypeStruct((t, D_MODEL), F32),
        scratch_shapes=[pltpu.VMEM((FFN_TILES_PER_STEP, TOKEN_TILE + 2 * FFN_HALO, D_MODEL), BF16),
                        pltpu.VMEM((FFN_TILES_PER_STEP, TOKEN_TILE, D_FF), BF16)],
        compiler_params=pltpu.CompilerParams(dimension_semantics=("parallel",),
                                             vmem_limit_bytes=VMEM_LIMIT_BYTES),
        name="ffn",
    )(h, h, h, lw["ffn_w_up"], lw["ffn_conv_w"], lw["ffn_conv_b"], lw["ffn_w_down"],
      lw["ln_ffn_g"], lw["ln_ffn_b"])


def _rope_tables(seq):
    rows = seq // GRID_W
    per_row = lambda x: jnp.repeat(x, GRID_W, axis=0)
    per_col = lambda x: jnp.tile(x, (rows, 1))

    def tables(half):
        inv_freq = ROPE_THETA ** (-jnp.arange(half, dtype=F32) / half)
        ar = jnp.arange(rows, dtype=F32)[:, None] * inv_freq[None, :]
        ac = jnp.arange(GRID_W, dtype=F32)[:, None] * inv_freq[None, :]
        cr, sr, cc, sc = per_row(jnp.cos(ar)), per_row(jnp.sin(ar)), per_col(jnp.cos(ac)), per_col(jnp.sin(ac))
        return jnp.concatenate([cr, cr, cc, cc], axis=-1), jnp.concatenate([-sr, sr, -sc, sc], axis=-1)

    cos64, sin64 = tables(HEAD_DIM // 4)
    cos32, sin32 = tables(MLA_ROPE // 4)
    pad = LANES - MLA_ROPE
    return {
        "cos_g": jnp.tile(cos64, (1, 2)), "sin_g": jnp.tile(sin64, (1, 2)),
        "cos_m": jnp.concatenate([cos32, jnp.ones((seq, pad), F32)], axis=-1),
        "sin_m": jnp.concatenate([sin32, jnp.zeros((seq, pad), F32)], axis=-1),
    }


def _layer_weights(l, w_in, conv_a_w, conv_a_b, ln_a_g, ln_a_b, qk_norm_q, qk_norm_k, sgu_ln_g, sgu_ln_b,
                   sgu_w, sgu_b, mla_q_norm, mla_w_uq, mla_kv_norm, mla_w_ukv, w_out, ln_mix_g, ln_mix_b,
                   ffn_w_up, ffn_conv_w, ffn_conv_b, ffn_w_down, ln_ffn_g, ln_ffn_b):
    w = w_in[l]
    z = lambda n: jnp.zeros((D_MODEL, n), F32)
    end_cq = COL_CQ + MLA_Q_LORA
    w_in_e = jnp.concatenate([w[:, :end_cq], z(2 * LANES - MLA_Q_LORA),
                              w[:, end_cq:], z(LANES - MLA_ROPE)], axis=1).astype(BF16)
    assert w.shape[1] == end_cq + MLA_KV_LORA + MLA_ROPE and w_in_e.shape == (D_MODEL, IN_COLS)

    dq = MLA_NOPE + MLA_ROPE
    zq = lambda n: jnp.zeros((MLA_Q_LORA, n), F32)
    cols = []
    for h in range(4):
        nope = mla_w_uq[l][:, h * dq:h * dq + MLA_NOPE]
        rope = mla_w_uq[l][:, h * dq + MLA_NOPE:(h + 1) * dq]
        cols += ([nope, zq(MLA_NOPE)] if h % 2 == 0 else [zq(MLA_NOPE), nope]) + [rope, zq(LANES - MLA_ROPE)]
    w_uq = jnp.concatenate(cols, axis=1)
    w_uq = jnp.concatenate([w_uq, jnp.zeros((2 * LANES - MLA_Q_LORA, w_uq.shape[1]), F32)], axis=0).astype(BF16)

    dkv = MLA_NOPE + HEAD_DIM
    w_ukv = jnp.concatenate([mla_w_ukv[l][:, h * dkv:h * dkv + MLA_NOPE] for h in range(4)]
                            + [mla_w_ukv[l][:, h * dkv + MLA_NOPE:(h + 1) * dkv] for h in range(4)],
                            axis=1).astype(BF16)

    row = lambda v: v.reshape(1, -1)
    return {
        "w_in_e": w_in_e,
        "gq": row(jnp.tile(qk_norm_q[l], 2)), "gk": row(jnp.tile(qk_norm_k[l], 2)),
        "sgu_g": row(sgu_ln_g[l]), "sgu_b": row(sgu_ln_b[l]),
        "sgu_w": sgu_w[l].reshape(SGU_GROUPS * CHUNK, CHUNK).astype(BF16),
        "sgu_bias": jnp.repeat(sgu_b[l].T, HEAD_DIM, axis=1),
        "gcq": row(jnp.concatenate([mla_q_norm[l], jnp.zeros((2 * LANES - MLA_Q_LORA,), F32)])),
        "w_uq": w_uq, "gckv": row(mla_kv_norm[l]), "w_ukv": w_ukv,
        "conv_a_w": conv_a_w[l], "conv_a_b": row(conv_a_b[l]), "ln_a_g": row(ln_a_g[l]), "ln_a_b": row(ln_a_b[l]),
        "layer": l, "w_out": w_out.astype(BF16), "ln_mix_g": row(ln_mix_g[l]), "ln_mix_b": row(ln_mix_b[l]),
        "ffn_w_up": ffn_w_up.astype(BF16), "ffn_conv_w": ffn_conv_w[l], "ffn_conv_b": row(ffn_conv_b[l]),
        "ffn_w_down": ffn_w_down.astype(BF16), "ln_ffn_g": row(ln_ffn_g[l]), "ln_ffn_b": row(ln_ffn_b[l]),
    }


def kernel(x, ln_in_g, ln_in_b, w_in, conv_a_w, conv_a_b, ln_a_g, ln_a_b, qk_norm_q, qk_norm_k, sgu_ln_g, sgu_ln_b, sgu_w, sgu_b, mla_q_norm, mla_w_uq, mla_kv_norm, mla_w_ukv, w_out, ln_mix_g, ln_mix_b, ffn_w_up, ffn_conv_w, ffn_conv_b, ffn_w_down, ln_ffn_g, ln_ffn_b):
    bsz, seq, d = x.shape
    assert d == D_MODEL and seq % TOKEN_TILE == 0 and seq % ATT_KEY_CHUNK == 0 and seq % GRID_W == 0
    t = bsz * seq
    tabs = _rope_tables(seq)
    h = x.reshape(t, d)
    for l in range(DEPTH):
        lw = _layer_weights(l, w_in, conv_a_w, conv_a_b, ln_a_g, ln_a_b, qk_norm_q, qk_norm_k, sgu_ln_g,
                            sgu_ln_b, sgu_w, sgu_b, mla_q_norm, mla_w_uq, mla_kv_norm, mla_w_ukv, w_out,
                            ln_mix_g, ln_mix_b, ffn_w_up, ffn_conv_w, ffn_conv_b, ffn_w_down, ln_ffn_g, ln_ffn_b)
        if l == 0:
            a, q_b, k_b, v_b, o_c, q_d, k_d, v_d, h = _inproj(
                h, seq, lw["w_in_e"], tabs, lw, input_ln=(ln_in_g.reshape(1, d), ln_in_b.reshape(1, d)))
        else:
            a, q_b, k_b, v_b, o_c, q_d, k_d, v_d = _inproj(h, seq, lw["w_in_e"], tabs, lw)
        sh = lambda arr: arr.reshape(bsz, seq, arr.shape[-1])
        o_b = _attention(sh(q_b), sh(k_b), sh(v_b), k_shared=True).reshape(t, GROUP_W)
        o_d = _attention(sh(q_d), sh(k_d), sh(v_d), k_shared=False).reshape(t, GROUP_W)
        h = _mixout(a, o_b, o_c, o_d, h, seq, lw)
        h = _ffn(h, seq, lw)
    return h.reshape(bsz, seq, d)
```

```python
from functools import partial

import jax
import jax.numpy as jnp
from jax import lax
from jax.experimental import pallas as pl
from jax.experimental.pallas import tpu as pltpu

F32 = jnp.float32
BF16 = jnp.bfloat16

D_MODEL = 1024
GRID_W = 64
GROUP_W = 256
HEAD_DIM = 64
CONV_A_WIDTH = 31
CHUNK = 128
SGU_GROUPS = 4
MLA_Q_LORA = 192
MLA_KV_LORA = 128
MLA_NOPE = 64
MLA_ROPE = 32
D_FF = 2816
DEPTH = 2
ROPE_THETA = 10000.0
DEEPNORM_ALPHA = (2 * DEPTH) ** 0.25
LN_EPS = 1e-5
RMS_EPS = 1e-6
LOG2_E = 1.4426950408889634

LANES = 128
F32_SUBLANES = 8
BF16_SUBLANES = 16
VMEM_LIMIT_BYTES = 52 * 1024 * 1024

TOKEN_TILE = 512
INPROJ_TILES_PER_STEP = 2
ATT_Q_TILE = 256
ATT_KEY_CHUNK = 1024
ATT_TILES_PER_STEP = 4
ATT_LOOKAHEAD = 2
FF_CHUNK = 256
FFN_TILES_PER_STEP = 1
FFN_LOOKAHEAD = 2
FFN_DOWN_AFTER = (5, 9, 11)
CONV_HALO = 16
FFN_HALO = F32_SUBLANES

COL_A = 0
COL_Q = 512
COL_KV = 768
COL_C = 1024
COL_CQ = 1536
COL_CKV_KR = 1792
IN_COLS = 2048


def _layer_slab(stacked, layer, pipeline_mode=None):
    return pl.BlockSpec((None,) + stacked.shape[1:], lambda i: (layer, 0, 0), pipeline_mode=pipeline_mode)


def _dot(a, b):
    return jnp.dot(a, b, preferred_element_type=F32)


def _layer_norm(x, g, b):
    mu = jnp.mean(x, axis=-1, keepdims=True)
    xc = x - mu
    var = jnp.mean(xc * xc, axis=-1, keepdims=True)
    return xc * lax.rsqrt(var + LN_EPS) * g + b


def _rope(x, cos, sin_signed, half):
    lane = lax.broadcasted_iota(jnp.int32, x.shape, 1)
    first = (lane & (2 * half - 1)) < half
    partner = jnp.where(first, pltpu.roll(x, LANES - half, 1), pltpu.roll(x, half, 1))
    return x * cos + partner * sin_signed


INPROJ_PER_TOKEN_REFS = (1, 2, 3, 4)
INPROJ_NUM_OUTPUTS = 8


def _inproj_tiles(block_rows, refs):
    first_out = len(refs) - INPROJ_NUM_OUTPUTS
    for start in range(0, block_rows, TOKEN_TILE):
        rows = pl.ds(start, TOKEN_TILE)
        yield rows, [r.at[rows] if (i in INPROJ_PER_TOKEN_REFS or i >= first_out) else r
                     for i, r in enumerate(refs)]


def _inproj_kernel(h_ref, *refs):
    for rows, tile_refs in _inproj_tiles(h_ref.shape[0], refs):
        _inproj_body(h_ref[rows, :], *tile_refs)


def _ln_inproj_kernel(x_ref, ln_g_ref, ln_b_ref, *refs):
    h_out_ref = refs[-1]
    for rows, tile_refs in _inproj_tiles(x_ref.shape[0], refs[:-1]):
        h = _layer_norm(x_ref[rows, :], ln_g_ref[...], ln_b_ref[...])
        h_out_ref[rows, :] = h
        _inproj_body(h, *tile_refs)


def _inproj_body(h, w_ref, cos_g_ref, sin_g_ref, cos_m_ref, sin_m_ref,
                 gq_ref, gk_ref, sgu_g_ref, sgu_b_ref, sgu_w_ref, sgu_bias_ref,
                 gcq_ref, w_uq_ref, gckv_ref, w_ukv_ref,
                 a_ref, qb_ref, kb_ref, vb_ref, oc_ref, qd_ref, kd_ref, vd_ref):
    tm = h.shape[0]
    hb = h.astype(BF16)

    def proj(col, width):
        return _dot(hb, w_ref[:, col:col + width])

    pc = proj(COL_C, 2 * GROUP_W)
    pcq = proj(COL_CQ, 2 * LANES)
    pckv_kr = proj(COL_CKV_KR, 2 * LANES)
    pckv, pkr = pckv_kr[:, :LANES], pckv_kr[:, LANES:]
    pq = proj(COL_Q, 2 * LANES)
    pkv = proj(COL_KV, 2 * LANES)
    pk, pv = pkv[:, :LANES], pkv[:, LANES:]

    pc = jax.nn.gelu(pc, approximate=True)
    u = pc[:, :GROUP_W]
    sv = _layer_norm(pc[:, GROUP_W:], sgu_g_ref[...], sgu_b_ref[...]).astype(BF16)
    gates = [_dot(sgu_w_ref[...], sv[n * CHUNK:(n + 1) * CHUNK, :]) for n in range(tm // CHUNK)]

    ms = jnp.sum(pcq * pcq, axis=-1, keepdims=True) * (1.0 / MLA_Q_LORA)
    cqn = (pcq * lax.rsqrt(ms + RMS_EPS) * gcq_ref[...]).astype(BF16)
    qds = [_dot(cqn, w_uq_ref[:, hd * 2 * LANES:(hd + 1) * 2 * LANES]) for hd in range(4)]
    ms = jnp.mean(pckv * pckv, axis=-1, keepdims=True)
    ckvn = (pckv * lax.rsqrt(ms + RMS_EPS) * gckv_ref[...]).astype(BF16)
    kvd = _dot(ckvn, w_ukv_ref[...])

    pa = proj(COL_A, 2 * GROUP_W)

    a_ref[...] = pa[:, :GROUP_W] * jax.nn.sigmoid(pa[:, GROUP_W:])

    cos_g, sin_g = cos_g_ref[...], sin_g_ref[...]
    cos_m, sin_m = cos_m_ref[...], sin_m_ref[...]
    low = lax.broadcasted_iota(jnp.int32, (tm, LANES), 1) < HEAD_DIM

    def norm_rope_heads(x, gain):
        sq = x * x
        ms_lo = jnp.sum(jnp.where(low, sq, 0.0), axis=-1, keepdims=True) * (1.0 / HEAD_DIM)
        ms_hi = jnp.sum(jnp.where(low, 0.0, sq), axis=-1, keepdims=True) * (1.0 / HEAD_DIM)
        rs = jnp.where(low, lax.rsqrt(ms_lo + RMS_EPS), lax.rsqrt(ms_hi + RMS_EPS))
        return _rope(x * rs * gain, cos_g, sin_g, HEAD_DIM // 4)

    for kv in range(2):
        xr = norm_rope_heads(pq[:, kv * LANES:(kv + 1) * LANES], gq_ref[...]) * (HEAD_DIM ** -0.5 * LOG2_E)
        swapped = pltpu.roll(xr, HEAD_DIM, 1)
        first, second = (xr, swapped) if kv == 0 else (swapped, xr)
        keep = low if kv == 0 else jnp.logical_not(low)
        qb_ref[:, (2 * kv) * LANES:(2 * kv + 1) * LANES] = jnp.where(keep, first, 0.0).astype(BF16)
        qb_ref[:, (2 * kv + 1) * LANES:(2 * kv + 2) * LANES] = jnp.where(keep, second, 0.0).astype(BF16)

    kb_ref[...] = norm_rope_heads(pk, gk_ref[...]).astype(BF16)
    ones = jnp.ones((tm, LANES), BF16)
    pv_swapped = pltpu.roll(pv, HEAD_DIM, 1)
    vb_ref[:, :LANES] = jnp.where(low, pv, pv_swapped).astype(BF16)
    vb_ref[:, LANES:2 * LANES] = ones
    vb_ref[:, 2 * LANES:3 * LANES] = jnp.where(low, pv_swapped, pv).astype(BF16)
    vb_ref[:, 3 * LANES:] = ones

    lane = lax.broadcasted_iota(jnp.int32, (CHUNK, GROUP_W), 1)
    for n, r in enumerate(gates):
        rows = slice(n * CHUNK, (n + 1) * CHUNK)
        gate = r[3 * CHUNK:4 * CHUNK, :]
        for g in (2, 1, 0):
            gate = jnp.where(lane < (g + 1) * HEAD_DIM, r[g * CHUNK:(g + 1) * CHUNK, :], gate)
        oc_ref[rows, :] = (u[rows, :] * (gate + sgu_bias_ref[...])).astype(BF16)

    scale_d = (MLA_NOPE + MLA_ROPE) ** -0.5 * LOG2_E
    for hd, qd in enumerate(qds):
        qd_ref[:, hd * 2 * LANES:hd * 2 * LANES + LANES] = (qd[:, :LANES] * scale_d).astype(BF16)
        qr = _rope(qd[:, LANES:], cos_m, sin_m, MLA_ROPE // 4) * scale_d
        qd_ref[:, hd * 2 * LANES + LANES:(hd + 1) * 2 * LANES] = qr.astype(BF16)
    kr = _rope(pkr, cos_m, sin_m, MLA_ROPE // 4).astype(BF16)
    for pr in range(2):
        kd_ref[:, pr * 2 * LANES:pr * 2 * LANES + LANES] = kvd[:, pr * LANES:(pr + 1) * LANES].astype(BF16)
        kd_ref[:, pr * 2 * LANES + LANES:(pr + 1) * 2 * LANES] = kr
    for pr in range(2):
        vd_ref[:, pr * 2 * LANES:pr * 2 * LANES + LANES] = kvd[:, (2 + pr) * LANES:(3 + pr) * LANES].astype(BF16)
        vd_ref[:, pr * 2 * LANES + LANES:(pr + 1) * 2 * LANES] = ones


def _inproj(h, seq, w_in_e, tabs, lw, input_ln=None):
    t = h.shape[0]
    tm = TOKEN_TILE * INPROJ_TILES_PER_STEP
    spt = seq // tm
    tok = lambda width: pl.BlockSpec((tm, width), lambda i: (i, 0))
    pos = pl.BlockSpec((tm, LANES), lambda i: (i % spt, 0))
    full = lambda arr: pl.BlockSpec(arr.shape, lambda i: (0,) * arr.ndim)
    consts = [lw["gq"], lw["gk"], lw["sgu_g"], lw["sgu_b"], lw["sgu_w"], lw["sgu_bias"],
              lw["gcq"], lw["w_uq"], lw["gckv"], lw["w_ukv"]]
    out_widths = [(GROUP_W, F32), (4 * LANES, BF16), (LANES, BF16), (4 * LANES, BF16), (GROUP_W, BF16),
                  (8 * LANES, BF16), (4 * LANES, BF16), (4 * LANES, BF16)]
    ln_args = [] if input_ln is None else list(input_ln)
    if input_ln is not None:
        out_widths = out_widths + [(D_MODEL, F32)]
    return pl.pallas_call(
        _inproj_kernel if input_ln is None else _ln_inproj_kernel,
        grid=(t // tm,),
        in_specs=[tok(D_MODEL)] + [full(c) for c in ln_args] + [full(w_in_e), pos, pos, pos, pos]
                 + [full(c) for c in consts],
        out_specs=[tok(w) for w, _ in out_widths],
        out_shape=[jax.ShapeDtypeStruct((t, w), dt) for w, dt in out_widths],
        compiler_params=pltpu.CompilerParams(dimension_semantics=("parallel",),
                                             vmem_limit_bytes=VMEM_LIMIT_BYTES),
        name="inproj" if input_ln is None else "ln_inproj",
    )(h, *ln_args, w_in_e, tabs["cos_g"], tabs["sin_g"], tabs["cos_m"], tabs["sin_m"], *consts)


def _attention_kernel(q_ref, k_ref, v_ref, o_ref):
    tq = ATT_Q_TILE
    kw = k_ref.shape[2]
    seq = k_ref.shape[1]
    nchunk = seq // ATT_KEY_CHUNK
    groups = ATT_KEY_CHUNK // LANES
    lane = lax.broadcasted_iota(jnp.int32, (tq, LANES), 1)
    items = [(t, c) for t in range(q_ref.shape[1] // tq) for c in range(nchunk)]

    def scores(item):
        t, c = item
        rows = slice(t * tq, (t + 1) * tq)
        q2 = jnp.concatenate([q_ref[0, rows, :kw], q_ref[0, rows, kw:]], axis=0)
        keys = k_ref[0, c * ATT_KEY_CHUNK:(c + 1) * ATT_KEY_CHUNK, :]
        return lax.dot_general(q2, keys, (((1,), (1,)), ((), ())), preferred_element_type=F32)

    ss = [scores(item) for item in items[:ATT_LOOKAHEAD]]
    m = acc = None
    for idx, (t, c) in enumerate(items):
        if idx + ATT_LOOKAHEAD < len(items):
            ss.append(scores(items[idx + ATT_LOOKAHEAD]))
        s = ss[idx]
        mc = s[:, :LANES]
        for g in range(1, groups):
            mc = jnp.maximum(mc, s[:, g * LANES:(g + 1) * LANES])
        mc = jnp.max(mc, axis=-1, keepdims=True)
        m_new = mc if c == 0 else jnp.maximum(m, mc)
        p = jnp.exp2(s - m_new).astype(BF16)
        pv = _dot(p, v_ref[0, c * ATT_KEY_CHUNK:(c + 1) * ATT_KEY_CHUNK, :])
        acc = pv if c == 0 else jnp.exp2(m - m_new) * acc + pv
        m = m_new
        if c == nchunk - 1:
            out = acc[:, :LANES] / acc[:, LANES:]
            o_ref[0, t * tq:(t + 1) * tq, :] = jnp.where(lane < HEAD_DIM, out[:tq], out[tq:]).astype(o_ref.dtype)


def _attention(q, k, v, k_shared):
    bsz, seq, _ = q.shape
    kw = k.shape[2] if k_shared else k.shape[2] // 2
    tq = ATT_Q_TILE * ATT_TILES_PER_STEP
    k_map = (lambda b, p, i: (b, 0, 0)) if k_shared else (lambda b, p, i: (b, 0, p))
    return pl.pallas_call(
        _attention_kernel,
        grid=(bsz, 2, seq // tq),
        in_specs=[pl.BlockSpec((1, tq, 2 * kw), lambda b, p, i: (b, i, p)),
                  pl.BlockSpec((1, seq, kw), k_map),
                  pl.BlockSpec((1, seq, 2 * LANES), lambda b, p, i: (b, 0, p))],
        out_specs=pl.BlockSpec((1, tq, LANES), lambda b, p, i: (b, i, p)),
        out_shape=jax.ShapeDtypeStruct((bsz, seq, 2 * LANES), BF16),
        compiler_params=pltpu.CompilerParams(dimension_semantics=("parallel", "parallel", "parallel"),
                                             vmem_limit_bytes=VMEM_LIMIT_BYTES),
        name="attention_shared_k" if k_shared else "attention",
    )(q, k, v)


def _mixout_kernel(a_ref, a_prev_ref, a_next_ref, cw_ref, cb_ref, ga_ref, ba_ref,
                   ob_ref, oc_ref, od_ref, w_ref, h_ref, g_ref, b_ref, o_ref, ext_ref, shift_ref,
                   *, tiles_per_seq):
    tm = a_ref.shape[0]
    i = pl.program_id(0) % tiles_per_seq
    has_prev = jnp.where(i > 0, 1.0, 0.0).astype(F32)
    has_next = jnp.where(i < tiles_per_seq - 1, 1.0, 0.0).astype(F32)
    ext_ref[0:CONV_HALO, :] = a_prev_ref[...] * has_prev
    ext_ref[CONV_HALO:CONV_HALO + tm, :] = a_ref[...]
    ext_ref[CONV_HALO + tm:, :] = a_next_ref[...] * has_next
    mix = _dot(ob_ref[...], w_ref[GROUP_W:2 * GROUP_W, :])
    mix = mix + _dot(oc_ref[...], w_ref[2 * GROUP_W:3 * GROUP_W, :])
    mix = mix + _dot(od_ref[...], w_ref[3 * GROUP_W:, :])

    half = CONV_A_WIDTH // 2
    span = tm + 2 * CONV_HALO - F32_SUBLANES
    acc = None
    for r in range(F32_SUBLANES):
        shift_ref[r] = ext_ref[r:r + span, :]
        for k in range(CONV_A_WIDTH):
            start = CONV_HALO - half + k
            if start % F32_SUBLANES == r:
                term = shift_ref[r, start - r:start - r + tm, :] * cw_ref[k:k + 1, :]
                acc = term if acc is None else acc + term
    conv = acc + cb_ref[...]
    o_a = jax.nn.silu(_layer_norm(conv, ga_ref[...], ba_ref[...])).astype(BF16)

    mix = mix + _dot(o_a, w_ref[0:GROUP_W, :])
    o_ref[...] = _layer_norm(DEEPNORM_ALPHA * h_ref[...] + mix, g_ref[...], b_ref[...])


def _mixout(a, o_b, o_c, o_d, h, seq, lw):
    t = h.shape[0]
    tm = TOKEN_TILE
    spt = seq // tm
    hb = tm // CONV_HALO
    last = t // CONV_HALO - 1
    tok = lambda width: pl.BlockSpec((tm, width), lambda i: (i, 0))
    full = lambda arr: pl.BlockSpec(arr.shape, lambda i: (0,) * arr.ndim)
    consts1 = [lw["conv_a_w"], lw["conv_a_b"], lw["ln_a_g"], lw["ln_a_b"]]
    consts2 = [lw["ln_mix_g"], lw["ln_mix_b"]]
    return pl.pallas_call(
        partial(_mixout_kernel, tiles_per_seq=spt),
        grid=(t // tm,),
        in_specs=[tok(GROUP_W),
                  pl.BlockSpec((CONV_HALO, GROUP_W), lambda i: (jnp.maximum(i * hb - 1, 0), 0)),
                  pl.BlockSpec((CONV_HALO, GROUP_W), lambda i: (jnp.minimum((i + 1) * hb, last), 0))]
                 + [full(c) for c in consts1]
                 + [tok(GROUP_W), tok(GROUP_W), tok(GROUP_W), _layer_slab(lw["w_out"], lw["layer"]), tok(D_MODEL)]
                 + [full(c) for c in consts2],
        out_specs=tok(D_MODEL),
        out_shape=jax.ShapeDtypeStruct((t, D_MODEL), F32),
        scratch_shapes=[pltpu.VMEM((tm + 2 * CONV_HALO, GROUP_W), F32),
                        pltpu.VMEM((F32_SUBLANES, tm + 2 * CONV_HALO - F32_SUBLANES, GROUP_W), F32)],
        compiler_params=pltpu.CompilerParams(dimension_semantics=("parallel",),
                                             vmem_limit_bytes=VMEM_LIMIT_BYTES),
        name="mixout",
    )(a, a, a, *consts1, o_b, o_c, o_d, lw["w_out"], h, *consts2)


def _ffn_kernel(h_ref, h_prev_ref, h_next_ref, w_up_ref, cw_ref, cb_ref, w_down_ref, g_ref, b_ref,
                o_ref, ext_ref, act_ref, *, blocks_per_seq):
    tm = TOKEN_TILE
    ntile = h_ref.shape[0] // tm
    i = pl.program_id(0) % blocks_per_seq
    has_prev = jnp.where(i > 0, 1.0, 0.0).astype(F32)
    has_next = jnp.where(i < blocks_per_seq - 1, 1.0, 0.0).astype(F32)
    for t in range(ntile):
        before = h_ref[t * tm - FFN_HALO:t * tm, :] if t > 0 else h_prev_ref[...] * has_prev
        after = h_ref[(t + 1) * tm:(t + 1) * tm + FFN_HALO, :] if t + 1 < ntile else h_next_ref[...] * has_next
        rows = pl.ds(t * tm, tm)
        _ffn_tile(h_ref[rows, :], before, after, w_up_ref, cw_ref, cb_ref, w_down_ref, g_ref, b_ref,
                  o_ref.at[rows], ext_ref.at[t], act_ref.at[t])


def _ffn_tile(h, before, after, w_up_ref, cw_ref, cb_ref, w_down_ref, g_ref, b_ref, o_ref, ext_ref, act_ref):
    tm = h.shape[0]
    ext_ref[0:tm, :] = h.astype(BF16)
    ext_ref[tm:, :] = jnp.concatenate([before, after], axis=0).astype(BF16)
    hext = ext_ref[...]
    row8 = lax.broadcasted_iota(jnp.int32, (F32_SUBLANES, FF_CHUNK), 0)

    def up_proj(col):
        return _dot(hext, w_up_ref[:, col:col + FF_CHUNK])

    def conv(up, col):
        mid = up[:tm]
        before = tm + F32_SUBLANES - 1
        after = tm + F32_SUBLANES
        dn = pltpu.roll(mid, 1, 0)
        dn = jnp.concatenate([jnp.where(row8 == 0, up[before:before + 1], dn[:F32_SUBLANES]),
                              dn[F32_SUBLANES:]], axis=0)
        nx = pltpu.roll(mid, tm - 1, 0)
        nx = jnp.concatenate([nx[:tm - F32_SUBLANES],
                              jnp.where(row8 == F32_SUBLANES - 1, up[after:after + 1], nx[tm - F32_SUBLANES:])],
                             axis=0)
        w = cw_ref[:, col:col + FF_CHUNK]
        return dn * w[0:1] + mid * w[1:2] + nx * w[2:3] + cb_ref[:, col:col + FF_CHUNK]

    nchunk = D_FF // FF_CHUNK
    ups = [(up_proj(j * FF_CHUNK), up_proj(D_FF + j * FF_CHUNK)) for j in range(FFN_LOOKAHEAD)]
    down = None
    done = 0
    for j in range(nchunk):
        col = j * FF_CHUNK
        if j + FFN_LOOKAHEAD < nchunk:
            ahead = col + FFN_LOOKAHEAD * FF_CHUNK
            ups.append((up_proj(ahead), up_proj(D_FF + ahead)))
        cur = ups[j]
        act_ref[:, col:col + FF_CHUNK] = (jax.nn.silu(conv(cur[0], col)) * conv(cur[1], D_FF + col)).astype(BF16)
        if j + 1 in FFN_DOWN_AFTER:
            lo, hi = done * FF_CHUNK, (j + 1) * FF_CHUNK
            part = _dot(act_ref[:, lo:hi], w_down_ref[lo:hi, :])
            down = part if down is None else down + part
            done = j + 1
    o_ref[...] = _layer_norm(DEEPNORM_ALPHA * h + down, g_ref[...], b_ref[...])


def _ffn(h, seq, lw):
    t = h.shape[0]
    tm = TOKEN_TILE * FFN_TILES_PER_STEP
    spt = seq // tm
    hb = tm // FFN_HALO
    last = t // FFN_HALO - 1
    tok = pl.BlockSpec((tm, D_MODEL), lambda i: (i, 0))
    full = lambda arr: pl.BlockSpec(arr.shape, lambda i: (0,) * arr.ndim)
    return pl.pallas_call(
        partial(_ffn_kernel, blocks_per_seq=spt),
        grid=(t // tm,),
        in_specs=[tok,
                  pl.BlockSpec((FFN_HALO, D_MODEL), lambda i: (jnp.maximum(i * hb - 1, 0), 0)),
                  pl.BlockSpec((FFN_HALO, D_MODEL), lambda i: (jnp.minimum((i + 1) * hb, last), 0)),
                  _layer_slab(lw["ffn_w_up"], lw["layer"], pl.Buffered(1)),
                  full(lw["ffn_conv_w"]), full(lw["ffn_conv_b"]),
                  _layer_slab(lw["ffn_w_down"], lw["layer"], pl.Buffered(1)),
                  full(lw["ln_ffn_g"]), full(lw["ln_ffn_b"])],
        out_specs=tok,
        out_shape=jax.ShapeDtypeStruct((t, D_MODEL), F32),
        scratch_shapes=[pltpu.VMEM((FFN_TILES_PER_STEP, TOKEN_TILE + 2 * FFN_HALO, D_MODEL), BF16),
                        pltpu.VMEM((FFN_TILES_PER_STEP, TOKEN_TILE, D_FF), BF16)],
        compiler_params=pltpu.CompilerParams(dimension_semantics=("parallel",),
                                             vmem_limit_bytes=VMEM_LIMIT_BYTES),
        name="ffn",
    )(h, h, h, lw["ffn_w_up"], lw["ffn_conv_w"], lw["ffn_conv_b"], lw["ffn_w_down"],
      lw["ln_ffn_g"], lw["ln_ffn_b"])


def _rope_tables(seq):
    rows = seq // GRID_W
    per_row = lambda x: jnp.repeat(x, GRID_W, axis=0)
    per_col = lambda x: jnp.tile(x, (rows, 1))

    def tables(half):
        inv_freq = ROPE_THETA ** (-jnp.arange(half, dtype=F32) / half)
        ar = jnp.arange(rows, dtype=F32)[:, None] * inv_freq[None, :]
        ac = jnp.arange(GRID_W, dtype=F32)[:, None] * inv_freq[None, :]
        cr, sr, cc, sc = per_row(jnp.cos(ar)), per_row(jnp.sin(ar)), per_col(jnp.cos(ac)), per_col(jnp.sin(ac))
        return jnp.concatenate([cr, cr, cc, cc], axis=-1), jnp.concatenate([-sr, sr, -sc, sc], axis=-1)

    cos64, sin64 = tables(HEAD_DIM // 4)
    cos32, sin32 = tables(MLA_ROPE // 4)
    pad = LANES - MLA_ROPE
    return {
        "cos_g": jnp.tile(cos64, (1, 2)), "sin_g": jnp.tile(sin64, (1, 2)),
        "cos_m": jnp.concatenate([cos32, jnp.ones((seq, pad), F32)], axis=-1),
        "sin_m": jnp.concatenate([sin32, jnp.zeros((seq, pad), F32)], axis=-1),
    }


def _layer_weights(l, w_in, conv_a_w, conv_a_b, ln_a_g, ln_a_b, qk_norm_q, qk_norm_k, sgu_ln_g, sgu_ln_b,
                   sgu_w, sgu_b, mla_q_norm, mla_w_uq, mla_kv_norm, mla_w_ukv, w_out, ln_mix_g, ln_mix_b,
                   ffn_w_up, ffn_conv_w, ffn_conv_b, ffn_w_down, ln_ffn_g, ln_ffn_b):
    w = w_in[l]
    z = lambda n: jnp.zeros((D_MODEL, n), F32)
    end_cq = COL_CQ + MLA_Q_LORA
    w_in_e = jnp.concatenate([w[:, :end_cq], z(2 * LANES - MLA_Q_LORA),
                              w[:, end_cq:], z(LANES - MLA_ROPE)], axis=1).astype(BF16)
    assert w.shape[1] == end_cq + MLA_KV_LORA + MLA_ROPE and w_in_e.shape == (D_MODEL, IN_COLS)

    dq = MLA_NOPE + MLA_ROPE
    zq = lambda n: jnp.zeros((MLA_Q_LORA, n), F32)
    cols = []
    for h in range(4):
        nope = mla_w_uq[l][:, h * dq:h * dq + MLA_NOPE]
        rope = mla_w_uq[l][:, h * dq + MLA_NOPE:(h + 1) * dq]
        cols += ([nope, zq(MLA_NOPE)] if h % 2 == 0 else [zq(MLA_NOPE), nope]) + [rope, zq(LANES - MLA_ROPE)]
    w_uq = jnp.concatenate(cols, axis=1)
    w_uq = jnp.concatenate([w_uq, jnp.zeros((2 * LANES - MLA_Q_LORA, w_uq.shape[1]), F32)], axis=0).astype(BF16)

    dkv = MLA_NOPE + HEAD_DIM
    w_ukv = jnp.concatenate([mla_w_ukv[l][:, h * dkv:h * dkv + MLA_NOPE] for h in range(4)]
                            + [mla_w_ukv[l][:, h * dkv + MLA_NOPE:(h + 1) * dkv] for h in range(4)],
                            axis=1).astype(BF16)

    row = lambda v: v.reshape(1, -1)
    return {
        "w_in_e": w_in_e,
        "gq": row(jnp.tile(qk_norm_q[l], 2)), "gk": row(jnp.tile(qk_norm_k[l], 2)),
        "sgu_g": row(sgu_ln_g[l]), "sgu_b": row(sgu_ln_b[l]),
        "sgu_w": sgu_w[l].reshape(SGU_GROUPS * CHUNK, CHUNK).astype(BF16),
        "sgu_bias": jnp.repeat(sgu_b[l].T, HEAD_DIM, axis=1),
        "gcq": row(jnp.concatenate([mla_q_norm[l], jnp.zeros((2 * LANES - MLA_Q_LORA,), F32)])),
        "w_uq": w_uq, "gckv": row(mla_kv_norm[l]), "w_ukv": w_ukv,
        "conv_a_w": conv_a_w[l], "conv_a_b": row(conv_a_b[l]), "ln_a_g": row(ln_a_g[l]), "ln_a_b": row(ln_a_b[l]),
        "layer": l, "w_out": w_out.astype(BF16), "ln_mix_g": row(ln_mix_g[l]), "ln_mix_b": row(ln_mix_b[l]),
        "ffn_w_up": ffn_w_up.astype(BF16), "ffn_conv_w": ffn_conv_w[l], "ffn_conv_b": row(ffn_conv_b[l]),
        "ffn_w_down": ffn_w_down.astype(BF16), "ln_ffn_g": row(ln_ffn_g[l]), "ln_ffn_b": row(ln_ffn_b[l]),
    }


def kernel(x, ln_in_g, ln_in_b, w_in, conv_a_w, conv_a_b, ln_a_g, ln_a_b, qk_norm_q, qk_norm_k, sgu_ln_g, sgu_ln_b, sgu_w, sgu_b, mla_q_norm, mla_w_uq, mla_kv_norm, mla_w_ukv, w_out, ln_mix_g, ln_mix_b, ffn_w_up, ffn_conv_w, ffn_conv_b, ffn_w_down, ln_ffn_g, ln_ffn_b):
    bsz, seq, d = x.shape
    assert d == D_MODEL and seq % TOKEN_TILE == 0 and seq % ATT_KEY_CHUNK == 0 and seq % GRID_W == 0
    t = bsz * seq
    tabs = _rope_tables(seq)
    h = x.reshape(t, d)
    for l in range(DEPTH):
        lw = _layer_weights(l, w_in, conv_a_w, conv_a_b, ln_a_g, ln_a_b, qk_norm_q, qk_norm_k, sgu_ln_g,
                            sgu_ln_b, sgu_w, sgu_b, mla_q_norm, mla_w_uq, mla_kv_norm, mla_w_ukv, w_out,
                            ln_mix_g, ln_mix_b, ffn_w_up, ffn_conv_w, ffn_conv_b, ffn_w_down, ln_ffn_g, ln_ffn_b)
        if l == 0:
            a, q_b, k_b, v_b, o_c, q_d, k_d, v_d, h = _inproj(
                h, seq, lw["w_in_e"], tabs, lw, input_ln=(ln_in_g.reshape(1, d), ln_in_b.reshape(1, d)))
        else:
            a, q_b, k_b, v_b, o_c, q_d, k_d, v_d = _inproj(h, seq, lw["w_in_e"], tabs, lw)
        sh = lambda arr: arr.reshape(bsz, seq, arr.shape[-1])
        o_b = _attention(sh(q_b), sh(k_b), sh(v_b), k_shared=True).reshape(t, GROUP_W)
        o_d = _attention(sh(q_d), sh(k_d), sh(v_d), k_shared=False).reshape(t, GROUP_W)
        h = _mixout(a, o_b, o_c, o_d, h, seq, lw)
        h = _ffn(h, seq, lw)
    return h.reshape(bsz, seq, d)
```
